```python
import math
import jax, jax.numpy as jnp
from jax import lax
import numpy as np

D_MODEL = 2048
BATCH = 8
SEQ = 2048
DEPTH = 1

CHUNK = 64
Q_BLOCK = 128
EPS = 1e-6
MIX_WIDTH = D_MODEL
DA_HEADS = 8
DA_V_DIM = (MIX_WIDTH // 2) // DA_HEADS
DA_QK_DIM = DA_V_DIM // 2
DA_WIDTH = DA_HEADS * DA_V_DIM
DA_QK_WIDTH = DA_HEADS * 2 * DA_QK_DIM
ML_HEADS = 4
ML_HEAD_DIM = (MIX_WIDTH // 2) // ML_HEADS
ML_WIDTH = ML_HEADS * ML_HEAD_DIM
CONV_K = 4
D_FF = 4 * D_MODEL
PLE_DIM = 256
SPLIT_SIZES = (DA_QK_WIDTH, DA_QK_WIDTH, DA_WIDTH,
               ML_WIDTH, ML_WIDTH, ML_WIDTH, ML_WIDTH, ML_HEADS, ML_HEADS,
               D_MODEL, D_MODEL)
IN_COLS = sum(SPLIT_SIZES)
NEG_INF = -1e30

kernel_name = 'hybrid_diffattn_mlstm_block'


def rmsnorm(x, g):
    xf = x.astype(jnp.float32)
    y = xf * lax.rsqrt(jnp.mean(xf * xf, axis=-1, keepdims=True) + EPS)
    return (y * g.astype(jnp.float32)).astype(x.dtype)


def diff_attention(q, k, v, lam_q1, lam_k1, lam_q2, lam_k2, sub_g, lambda_init):
    bsz, seq = q.shape[0], q.shape[1]
    qf = q.astype(jnp.float32).reshape(bsz, seq, DA_HEADS, 2, DA_QK_DIM) * (DA_QK_DIM ** -0.5)
    kf = k.astype(jnp.float32).reshape(bsz, seq, DA_HEADS, 2, DA_QK_DIM)
    vf = v.astype(jnp.float32).reshape(bsz, seq, DA_HEADS, DA_V_DIM)
    f32 = jnp.float32
    lam = (jnp.exp(jnp.sum(lam_q1.astype(f32) * lam_k1.astype(f32)))
           - jnp.exp(jnp.sum(lam_q2.astype(f32) * lam_k2.astype(f32))) + lambda_init)
    outs = []
    for blk in range(seq // Q_BLOCK):
        q0 = blk * Q_BLOCK
        k_end = q0 + Q_BLOCK
        s = jnp.einsum('bqhmd,bkhmd->bhmqk', qf[:, q0:k_end], kf[:, :k_end])
        mask = (np.arange(k_end)[None, :] // CHUNK) <= (np.arange(q0, k_end)[:, None] // CHUNK)
        a = jax.nn.softmax(jnp.where(mask, s, NEG_INF), axis=-1)
        w = a[:, :, 0] - lam * a[:, :, 1]
        outs.append(jnp.einsum('bhqk,bkhd->bqhd', w, vf[:, :k_end]))
    o = jnp.concatenate(outs, axis=1)
    o = rmsnorm(o, sub_g) * (1.0 - lambda_init)
    return o.reshape(bsz, seq, DA_WIDTH).astype(q.dtype)


def causal_conv_silu(x, w, b):
    seq = x.shape[1]
    xp = jnp.pad(x, ((0, 0), (CONV_K - 1, 0), (0, 0)))
    y = b + sum(xp[:, j:j + seq] * w[j] for j in range(CONV_K))
    return jax.nn.silu(y)


def _to_chunks(t):
    bsz, seq, h = t.shape[0], t.shape[1], t.shape[2]
    t = jnp.swapaxes(t, 1, 2).reshape(bsz, h, seq // CHUNK, CHUNK, *t.shape[3:])
    return jnp.moveaxis(t, 2, 0)


def mlstm_chunkwise(q, k, v, log_i, log_f):
    bsz, seq, nh, d = q.shape
    qc = _to_chunks(q * (d ** -0.5))
    kc = _to_chunks(k)
    vc = _to_chunks(v)
    ic = _to_chunks(log_i)
    bc = jnp.cumsum(_to_chunks(log_f), axis=-1)
    causal = np.tril(np.ones((CHUNK, CHUNK), dtype=bool))

    def step(carry, xs):
        c_prev, n_prev, m_prev = carry
        q_c, k_c, v_c, i_c, b_c = xs
        dlog = b_c[..., :, None] - b_c[..., None, :] + i_c[..., None, :]
        dlog = jnp.where(causal, dlog, NEG_INF)
        inter_log = b_c + m_prev[..., None]
        m_t = jnp.maximum(inter_log, jnp.max(dlog, axis=-1))
        dmat = jnp.exp(dlog - m_t[..., None])
        inter_w = jnp.exp(inter_log - m_t)
        s = jnp.einsum('bhld,bhsd->bhls', q_c, k_c) * dmat
        num = (jnp.einsum('bhls,bhsd->bhld', s, v_c)
               + inter_w[..., None] * jnp.einsum('bhld,bhde->bhle', q_c, c_prev))
        den = jnp.sum(s, axis=-1) + inter_w * jnp.einsum('bhld,bhd->bhl', q_c, n_prev)
        h_out = num / jnp.maximum(jnp.abs(den), jnp.exp(-m_t))[..., None]
        b_last = b_c[..., -1]
        upd_log = b_last[..., None] - b_c + i_c
        m_new = jnp.maximum(b_last + m_prev, jnp.max(upd_log, axis=-1))
        w_s = jnp.exp(upd_log - m_new[..., None])
        decay = jnp.exp(b_last + m_prev - m_new)
        c_new = decay[..., None, None] * c_prev + jnp.einsum('bhs,bhsd,bhse->bhde', w_s, k_c, v_c)
        n_new = decay[..., None] * n_prev + jnp.einsum('bhs,bhsd->bhd', w_s, k_c)
        return (c_new, n_new, m_new), h_out

    init = (jnp.zeros((bsz, nh, d, d), jnp.float32),
            jnp.zeros((bsz, nh, d), jnp.float32),
            jnp.zeros((bsz, nh), jnp.float32))
    _, hs = lax.scan(step, init, (qc, kc, vc, ic, bc))
    return jnp.transpose(hs, (1, 0, 3, 2, 4)).reshape(bsz, seq, nh, d)


def setup_inputs(seed: int = 0) -> dict:
    key = jax.random.key(seed)
    ks = jax.random.split(key, 24)
    f32 = jnp.float32

    def nrm(k, shape, scale):
        return jax.random.normal(k, shape, f32) * scale

    def gain(k, shape):
        return 1.0 + 0.05 * jax.random.normal(k, shape, f32)

    return {
        'x': nrm(ks[0], (BATCH, SEQ, D_MODEL), 1.0),
        'p': nrm(ks[1], (DEPTH, BATCH, SEQ, PLE_DIM), 1.0),
        'g_mix': gain(ks[2], (DEPTH, D_MODEL)),
        'w_in': nrm(ks[3], (DEPTH, D_MODEL, IN_COLS), D_MODEL ** -0.5),
        'conv_w': nrm(ks[4], (DEPTH, CONV_K, 2 * ML_WIDTH), CONV_K ** -0.5),
        'conv_b': nrm(ks[5], (DEPTH, 2 * ML_WIDTH), 0.02),
        'b_i': nrm(ks[6], (DEPTH, ML_HEADS), 0.1) - 1.0,
        'b_f': nrm(ks[7], (DEPTH, ML_HEADS), 0.5) + 3.0,
        'lam_q1': nrm(ks[8], (DEPTH, DA_QK_DIM), 0.1),
        'lam_k1': nrm(ks[9], (DEPTH, DA_QK_DIM), 0.1),
        'lam_q2': nrm(ks[10], (DEPTH, DA_QK_DIM), 0.1),
        'lam_k2': nrm(ks[11], (DEPTH, DA_QK_DIM), 0.1),
        'da_sub_g': gain(ks[12], (DEPTH, DA_V_DIM)),
        'ml_norm_g': gain(ks[13], (DEPTH, ML_WIDTH)),
        'w_pa': nrm(ks[14], (DEPTH, DA_WIDTH, D_MODEL), DA_WIDTH ** -0.5),
        'w_pb': nrm(ks[15], (DEPTH, ML_WIDTH, D_MODEL), ML_WIDTH ** -0.5),
        'w_o': nrm(ks[16], (DEPTH, D_MODEL, D_MODEL), D_MODEL ** -0.5),
        'g_mlp': gain(ks[17], (DEPTH, D_MODEL)),
        'w_up': nrm(ks[18], (DEPTH, D_MODEL, D_FF), D_MODEL ** -0.5),
        'w_down': nrm(ks[19], (DEPTH, D_FF, D_MODEL), D_FF ** -0.5),
        'g_ple': gain(ks[20], (DEPTH, D_MODEL)),
        'w_ple_gate': nrm(ks[21], (DEPTH, D_MODEL, D_MODEL), D_MODEL ** -0.5),
        'w_ple_proj': nrm(ks[22], (DEPTH, PLE_DIM, D_MODEL), PLE_DIM ** -0.5),
        'g_final': gain(ks[23], (D_MODEL,)),
    }


def reference(x, p, g_mix, w_in, conv_w, conv_b, b_i, b_f, lam_q1, lam_k1, lam_q2, lam_k2,
              da_sub_g, ml_norm_g, w_pa, w_pb, w_o, g_mlp, w_up, w_down, g_ple, w_ple_gate,
              w_ple_proj, g_final):
    bsz, seq, _ = x.shape
    f32 = jnp.float32
    split_at = np.cumsum(SPLIT_SIZES)[:-1].tolist()
    ml_shape = (bsz, seq, ML_HEADS, ML_HEAD_DIM)
    for i in range(DEPTH):
        lambda_init = 0.8 - 0.6 * math.exp(-0.3 * i)
        h = rmsnorm(x, g_mix[i])
        (a_q, a_k, a_v, m_q, m_k, m_v, m_o, m_i, m_f, gate_a, gate_b) = jnp.split(
            h @ w_in[i], split_at, axis=-1)
        y_a = diff_attention(a_q, a_k, a_v, lam_q1[i], lam_k1[i], lam_q2[i], lam_k2[i],
                             da_sub_g[i], lambda_init)
        qk = causal_conv_silu(jnp.concatenate([m_q, m_k], axis=-1), conv_w[i], conv_b[i])
        m_q, m_k = jnp.split(qk, 2, axis=-1)
        log_i = (m_i + b_i[i]).astype(f32)
        log_f = jax.nn.log_sigmoid((m_f + b_f[i]).astype(f32))
        hm = mlstm_chunkwise(m_q.astype(f32).reshape(ml_shape), m_k.astype(f32).reshape(ml_shape),
                             m_v.astype(f32).reshape(ml_shape), log_i, log_f)
        hm = rmsnorm(hm, ml_norm_g[i].reshape(ML_HEADS, ML_HEAD_DIM)).reshape(bsz, seq, ML_WIDTH)
        y_b = (jax.nn.sigmoid(m_o.astype(f32)) * hm).astype(x.dtype)
        merged = (jax.nn.sigmoid(gate_a) * (y_a @ w_pa[i])
                  + jax.nn.sigmoid(gate_b) * (y_b @ w_pb[i]))
        x = x + merged @ w_o[i]
        u = rmsnorm(x, g_mlp[i]) @ w_up[i]
        x = x + jnp.square(jax.nn.relu(u)) @ w_down[i]
        ple_gate = jax.nn.sigmoid(rmsnorm(x, g_ple[i]) @ w_ple_gate[i])
        x = x + ple_gate * (p[i] @ w_ple_proj[i])
    return rmsnorm(x, g_final)
```

```python
import functools
import math

import jax
import jax.numpy as jnp
from jax import lax
from jax.experimental import pallas as pl
from jax.experimental.pallas import tpu as pltpu

F32 = jnp.float32
BF16 = jnp.bfloat16

D_MODEL = 2048
CHUNK = 64
EPS = 1e-6
DA_HEADS = 8
DA_V_DIM = 128
DA_QK_DIM = 64
DA_WIDTH = DA_HEADS * DA_V_DIM
ML_HEADS = 4
ML_HEAD_DIM = 256
ML_WIDTH = ML_HEADS * ML_HEAD_DIM
CONV_K = 4
D_FF = 4 * D_MODEL
PLE_DIM = 256
NEG_INF = -1e30

LANES = 128
SUBLANES = 8
VMEM_LIMIT_BYTES = 56 * 1024 * 1024

ZB_AQ, ZB_AK, ZB_AV, ZB_MV, ZB_GA, ZB_GB, ZB_MO = 0, 1024, 2048, 3072, 4096, 6144, 8192
ZB_COLS = 9216
ZQK_COLS = 2 * ML_WIDTH


def _rms(x, g):
    ms = jnp.mean(x * x, axis=-1, keepdims=True)
    return (x * lax.rsqrt(ms + EPS)) * g


def _dot(a, b):
    return jnp.dot(a, b, preferred_element_type=F32)


def _dot_nt(a, b):
    return lax.dot_general(a, b, (((1,), (1,)), ((), ())), preferred_element_type=F32)


def _dot_tn(a, b):
    return lax.dot_general(a, b, (((0,), (0,)), ((), ())), preferred_element_type=F32)


def _params(*sem):
    return pltpu.CompilerParams(dimension_semantics=sem, vmem_limit_bytes=VMEM_LIMIT_BYTES)


def _in_proj_kernel(x_ref, g_ref, w_ref, wif_ref, zb_ref, zqk_ref, zif_ref, h_ref, *, nb_bf16):
    j = pl.program_id(1)

    @pl.when(j == 0)
    def _():
        hb = _rms(x_ref[...], g_ref[...]).astype(BF16)
        h_ref[...] = hb
        zif_ref[...] = _dot(hb, wif_ref[...])

    @pl.when(j < nb_bf16)
    def _():
        zb_ref[...] = _dot(h_ref[...], w_ref[...]).astype(BF16)

    @pl.when(j >= nb_bf16)
    def _():
        zqk_ref[...] = _dot(h_ref[...], w_ref[...])


def _in_proj(x2, g_mix, w_main, w_if, *, tm=512, tn=1024):
    t = x2.shape[0]
    nb_bf16 = ZB_COLS // tn
    nb = w_main.shape[1] // tn
    return pl.pallas_call(
        functools.partial(_in_proj_kernel, nb_bf16=nb_bf16),
        grid=(t // tm, nb),
        in_specs=[
            pl.BlockSpec((tm, D_MODEL), lambda i, j: (i, 0)),
            pl.BlockSpec((1, D_MODEL), lambda i, j: (0, 0)),
            pl.BlockSpec((D_MODEL, tn), lambda i, j: (0, j)),
            pl.BlockSpec((D_MODEL, LANES), lambda i, j: (0, 0)),
        ],
        out_specs=[
            pl.BlockSpec((tm, tn), lambda i, j: (i, jnp.minimum(j, nb_bf16 - 1))),
            pl.BlockSpec((tm, tn), lambda i, j: (i, jnp.maximum(j - nb_bf16, 0))),
            pl.BlockSpec((tm, LANES), lambda i, j: (i, 0)),
        ],
        out_shape=[
            jax.ShapeDtypeStruct((t, ZB_COLS), BF16),
            jax.ShapeDtypeStruct((t, ZQK_COLS), F32),
            jax.ShapeDtypeStruct((t, LANES), F32),
        ],
        scratch_shapes=[pltpu.VMEM((tm, D_MODEL), BF16)],
        compiler_params=_params("parallel", "arbitrary"),
        name="in_proj",
    )(x2, g_mix, w_main, w_if)


def _attn_kernel(lam_ref, q_ref, k_ref, v_ref, g_ref, o_ref,
                 m1_ref, l1_ref, a1_ref, m2_ref, l2_ref, a2_ref, *, tq, lambda_init):
    qi = pl.program_id(2)
    q = q_ref[...]
    lane = lax.broadcasted_iota(jnp.int32, q.shape, 1)
    zero = jnp.zeros_like(q)
    q_maps = (jnp.where(lane < DA_QK_DIM, q, zero), jnp.where(lane >= DA_QK_DIM, q, zero))
    states = ((m1_ref, l1_ref, a1_ref), (m2_ref, l2_ref, a2_ref))
    for m_ref, l_ref, a_ref in states:
        m_ref[...] = jnp.full(m_ref.shape, NEG_INF, F32)
        l_ref[...] = jnp.zeros(l_ref.shape, F32)
        a_ref[...] = jnp.zeros(a_ref.shape, F32)

    row = lax.broadcasted_iota(jnp.int32, (tq, tq), 0)
    col = lax.broadcasted_iota(jnp.int32, (tq, tq), 1)
    diag_mask = (col // CHUNK) <= (row // CHUNK)
    scale = DA_QK_DIM ** -0.5

    def block(kb, masked):
        start = pl.multiple_of(kb * tq, tq)
        k = k_ref[pl.ds(start, tq), :]
        v = v_ref[pl.ds(start, tq), :]
        for qm, (m_ref, l_ref, a_ref) in zip(q_maps, states):
            s = _dot_nt(qm, k) * scale
            if masked:
                s = jnp.where(diag_mask, s, NEG_INF)
            m_prev = m_ref[...]
            m_next = jnp.maximum(m_prev, jnp.max(s, axis=1, keepdims=True))
            alpha = jnp.exp(m_prev - m_next)
            p = jnp.exp(s - m_next[:, :1])
            l_ref[...] = alpha * l_ref[...] + jnp.sum(p, axis=1, keepdims=True)
            a_ref[...] = alpha * a_ref[...] + _dot(p.astype(BF16), v)
            m_ref[...] = m_next

    def body(kb, carry):
        block(kb, False)
        return carry

    lax.fori_loop(0, qi, body, 0)
    block(qi, True)

    lq = lam_ref[...]
    t1 = jnp.sum(lq[0:1, :] * lq[1:2, :], axis=-1, keepdims=True)
    t2 = jnp.sum(lq[2:3, :] * lq[3:4, :], axis=-1, keepdims=True)
    lam = jnp.exp(t1) - jnp.exp(t2) + lambda_init
    o = a1_ref[...] / l1_ref[...] - lam * (a2_ref[...] / l2_ref[...])
    o_ref[...] = (_rms(o, g_ref[...]) * (1.0 - lambda_init)).astype(o_ref.dtype)


def _attention(zb, lam4, sub_g, *, bsz, seq, lambda_init, tq=256):
    nq = seq // tq
    kq, kk, kv = ZB_AQ // DA_V_DIM, ZB_AK // DA_V_DIM, ZB_AV // DA_V_DIM
    stat = pltpu.VMEM((tq, LANES), F32)
    return pl.pallas_call(
        functools.partial(_attn_kernel, tq=tq, lambda_init=lambda_init),
        grid=(bsz, DA_HEADS, nq),
        in_specs=[
            pl.BlockSpec((4, DA_QK_DIM), lambda b, h, i: (0, 0)),
            pl.BlockSpec((tq, DA_V_DIM), lambda b, h, i: (b * nq + i, kq + h)),
            pl.BlockSpec((seq, DA_V_DIM), lambda b, h, i: (b, kk + h)),
            pl.BlockSpec((seq, DA_V_DIM), lambda b, h, i: (b, kv + h)),
            pl.BlockSpec((1, DA_V_DIM), lambda b, h, i: (0, 0)),
        ],
        out_specs=pl.BlockSpec((tq, DA_V_DIM), lambda b, h, i: (b * nq + i, h)),
        out_shape=jax.ShapeDtypeStruct((bsz * seq, DA_WIDTH), BF16),
        scratch_shapes=[stat, stat, stat, stat, stat, stat],
        compiler_params=_params("parallel", "parallel", "arbitrary"),
        name="diff_attn",
    )(lam4, zb, zb, zb, sub_g)


def _log_sigmoid(x):
    return jnp.minimum(x, 0.0) - jnp.log1p(jnp.exp(-jnp.abs(x)))


def _mlstm_kernel(bias_ref, qpre_ref, kpre_ref, v_ref, og_ref, zif_ref,
                  cwq_ref, cwk_ref, cbq_ref, cbk_ref, g_ref, y_ref, c_ref, *, seq):
    L = CHUNK
    d = ML_HEAD_DIM
    h = pl.program_id(1)
    b_i = bias_ref[h]
    b_f = bias_ref[ML_HEADS + h]
    c_ref[...] = jnp.zeros(c_ref.shape, F32)

    lane = lax.broadcasted_iota(jnp.int32, (L, LANES), 1)
    sel_i = lane == h
    sel_f = lane == h + ML_HEADS
    row = lax.broadcasted_iota(jnp.int32, (L, L), 0)
    col = lax.broadcasted_iota(jnp.int32, (L, L), 1)
    causal = col <= row
    eye = col == row
    qscale = d ** -0.5

    def conv_silu(pre_ref, w_ref, b_ref, c, t0):
        cur = pre_ref[pl.ds(t0, L), :]
        p0 = pl.multiple_of(jnp.maximum(t0 - SUBLANES, 0), SUBLANES)
        prev = jnp.where(c > 0, pre_ref[pl.ds(p0, SUBLANES), :], 0.0)
        xw = jnp.concatenate([prev, cur], axis=0)
        y = b_ref[...]
        for j in range(CONV_K):
            off = SUBLANES - (CONV_K - 1) + j
            y = y + w_ref[j:j + 1, :] * xw[off:off + L, :]
        return y * jax.nn.sigmoid(y)

    def chunk(c, carry):
        m_prev, n_prev = carry
        t0 = pl.multiple_of(c * L, L)
        z = zif_ref[pl.ds(t0, L), :]
        i_col = jnp.sum(jnp.where(sel_i, z, 0.0), axis=1, keepdims=True) + b_i
        f_col = _log_sigmoid(jnp.sum(jnp.where(sel_f, z, 0.0), axis=1, keepdims=True) + b_f)
        f_mat = jnp.broadcast_to(f_col, (L, L))
        i_mat = jnp.broadcast_to(i_col, (L, L))
        b_row = jnp.sum(jnp.where(row <= col, f_mat, 0.0), axis=0, keepdims=True)
        i_row = jnp.sum(jnp.where(eye, i_mat, 0.0), axis=0, keepdims=True)
        b_col = jnp.sum(jnp.where(eye, jnp.broadcast_to(b_row, (L, L)), 0.0), axis=1, keepdims=True)

        dlog = jnp.where(causal, b_col - b_row + i_row, NEG_INF)
        inter_log = b_col + m_prev
        m_t = jnp.maximum(inter_log, jnp.max(dlog, axis=1, keepdims=True))
        dmat = jnp.exp(dlog - m_t)
        inter_w = jnp.exp(inter_log - m_t)

        q = conv_silu(qpre_ref, cwq_ref, cbq_ref, c, t0)
        k = conv_silu(kpre_ref, cwk_ref, cbk_ref, c, t0)
        qb = q.astype(BF16)
        kb = k.astype(BF16)
        v = v_ref[pl.ds(t0, L), :]

        s = (_dot_nt(qb, kb) * qscale) * dmat
        inter = _dot(qb, c_ref[...].astype(BF16)) * qscale
        num = _dot(s.astype(BF16), v) + inter_w * inter
        qn = jnp.sum(q * n_prev, axis=1, keepdims=True) * qscale
        den = jnp.sum(s, axis=1, keepdims=True) + inter_w * qn
        hh = num / jnp.maximum(jnp.abs(den), jnp.exp(-m_t))

        hn = _rms(hh, g_ref[...])
        og = og_ref[pl.ds(t0, L), :].astype(F32)
        y_ref[pl.ds(t0, L), :] = (jax.nn.sigmoid(og) * hn).astype(y_ref.dtype)

        b_last = b_col[L - 1:L, :]
        upd = b_last - b_col + i_col
        m_new = jnp.maximum(b_last + m_prev, jnp.max(upd, axis=0, keepdims=True))
        w_col = jnp.exp(upd - m_new)
        decay = jnp.exp(b_last + m_prev - m_new)
        kw = k * w_col
        c_ref[...] = decay * c_ref[...] + _dot_tn(kw.astype(BF16), v)
        n_new = decay * n_prev + jnp.sum(kw, axis=0, keepdims=True)
        return m_new, n_new

    lax.fori_loop(0, seq // L, chunk, (jnp.zeros((1, 1), F32), jnp.zeros((1, d), F32)))


def _mlstm(zb, zqk, zif, bias_if, conv_w, conv_b, norm_g, *, bsz, seq):
    d = ML_HEAD_DIM
    kmv, kmo = ZB_MV // d, ZB_MO // d
    return pl.pallas_call(
        functools.partial(_mlstm_kernel, seq=seq),
        grid=(bsz, ML_HEADS),
        in_specs=[
            pl.BlockSpec(memory_space=pltpu.SMEM),
            pl.BlockSpec((seq, d), lambda b, h: (b, h)),
            pl.BlockSpec((seq, d), lambda b, h: (b, ML_HEADS + h)),
            pl.BlockSpec((seq, d), lambda b, h: (b, kmv + h)),
            pl.BlockSpec((seq, d), lambda b, h: (b, kmo + h)),
            pl.BlockSpec((seq, LANES), lambda b, h: (b, 0)),
            pl.BlockSpec((CONV_K, d), lambda b, h: (0, h)),
            pl.BlockSpec((CONV_K, d), lambda b, h: (0, ML_HEADS + h)),
            pl.BlockSpec((1, d), lambda b, h: (0, h)),
            pl.BlockSpec((1, d), lambda b, h: (0, ML_HEADS + h)),
            pl.BlockSpec((1, d), lambda b, h: (0, h)),
        ],
        out_specs=pl.BlockSpec((seq, d), lambda b, h: (b, h)),
        out_shape=jax.ShapeDtypeStruct((bsz * seq, ML_WIDTH), BF16),
        scratch_shapes=[pltpu.VMEM((d, d), F32)],
        compiler_params=_params("parallel", "parallel"),
        name="mlstm",
    )(bias_if, zqk, zqk, zb, zb, zif, conv_w, conv_w, conv_b, conv_b, norm_g)


def _merge_kernel(ya_ref, yb_ref, ga_ref, gb_ref, x_ref, wpa_ref, wpb_ref, wo_ref, o_ref):
    pa = _dot(ya_ref[...], wpa_ref[...])
    pb = _dot(yb_ref[...], wpb_ref[...])
    merged = (jax.nn.sigmoid(ga_ref[...].astype(F32)) * pa
              + jax.nn.sigmoid(gb_ref[...].astype(F32)) * pb)
    o_ref[...] = x_ref[...] + _dot(merged.astype(BF16), wo_ref[...])


def _resident(shape):
    return pl.BlockSpec(shape, lambda *_: (0,) * len(shape), pipeline_mode=pl.Buffered(1))


def _merge(ya, yb, zb, x2, w_pa, w_pb, w_o, *, tm=256):
    t = x2.shape[0]
    kga, kgb = ZB_GA // D_MODEL, ZB_GB // D_MODEL
    return pl.pallas_call(
        _merge_kernel,
        grid=(t // tm,),
        in_specs=[
            pl.BlockSpec((tm, DA_WIDTH), lambda i: (i, 0)),
            pl.BlockSpec((tm, ML_WIDTH), lambda i: (i, 0)),
            pl.BlockSpec((tm, D_MODEL), lambda i: (i, kga)),
            pl.BlockSpec((tm, D_MODEL), lambda i: (i, kgb)),
            pl.BlockSpec((tm, D_MODEL), lambda i: (i, 0)),
            _resident((DA_WIDTH, D_MODEL)),
            _resident((ML_WIDTH, D_MODEL)),
            _resident((D_MODEL, D_MODEL)),
        ],
        out_specs=pl.BlockSpec((tm, D_MODEL), lambda i: (i, 0)),
        out_shape=jax.ShapeDtypeStruct((t, D_MODEL), F32),
        compiler_params=_params("parallel"),
        name="merge",
    )(ya, yb, zb, zb, x2, w_pa, w_pb, w_o)


def _mlp_kernel(x_ref, g_ref, wup_ref, wdn_ref, o_ref, h_ref):
    f = pl.program_id(1)

    @pl.when(f == 0)
    def _():
        x = x_ref[...]
        h_ref[...] = _rms(x, g_ref[...]).astype(BF16)
        o_ref[...] = x

    u = _dot(h_ref[...], wup_ref[...])
    a = jnp.square(jnp.maximum(u, 0.0)).astype(BF16)
    o_ref[...] += _dot(a, wdn_ref[...])


def _mlp(x1, g_mlp, w_up, w_down, *, tm=512, tf=1024):
    t = x1.shape[0]
    return pl.pallas_call(
        _mlp_kernel,
        grid=(t // tm, D_FF // tf),
        in_specs=[
            pl.BlockSpec((tm, D_MODEL), lambda i, f: (i, 0)),
            pl.BlockSpec((1, D_MODEL), lambda i, f: (0, 0)),
            pl.BlockSpec((D_MODEL, tf), lambda i, f: (0, f)),
            pl.BlockSpec((tf, D_MODEL), lambda i, f: (f, 0)),
        ],
        out_specs=pl.BlockSpec((tm, D_MODEL), lambda i, f: (i, 0)),
        out_shape=jax.ShapeDtypeStruct((t, D_MODEL), F32),
        scratch_shapes=[pltpu.VMEM((tm, D_MODEL), BF16)],
        compiler_params=_params("parallel", "arbitrary"),
        name="mlp",
    )(x1, g_mlp, w_up, w_down)


def _ple_kernel(x_ref, p_ref, gp_ref, wg_ref, wp_ref, gf_ref, o_ref):
    x = x_ref[...]
    gate = jax.nn.sigmoid(_dot(_rms(x, gp_ref[...]).astype(BF16), wg_ref[...]))
    proj = _dot(p_ref[...].astype(BF16), wp_ref[...])
    o_ref[...] = _rms(x + gate * proj, gf_ref[...])


def _ple(x2, p2, g_ple, w_gate, w_proj, g_final, *, tm=256):
    t = x2.shape[0]
    return pl.pallas_call(
        _ple_kernel,
        grid=(t // tm,),
        in_specs=[
            pl.BlockSpec((tm, D_MODEL), lambda i: (i, 0)),
            pl.BlockSpec((tm, PLE_DIM), lambda i: (i, 0)),
            pl.BlockSpec((1, D_MODEL), lambda i: (0, 0)),
            _resident((D_MODEL, D_MODEL)),
            _resident((PLE_DIM, D_MODEL)),
            pl.BlockSpec((1, D_MODEL), lambda i: (0, 0)),
        ],
        out_specs=pl.BlockSpec((tm, D_MODEL), lambda i: (i, 0)),
        out_shape=jax.ShapeDtypeStruct((t, D_MODEL), F32),
        compiler_params=_params("parallel"),
        name="ple_final",
    )(x2, p2, g_ple, w_gate, w_proj, g_final)


def kernel(x, p, g_mix, w_in, conv_w, conv_b, b_i, b_f, lam_q1, lam_k1, lam_q2, lam_k2,
           da_sub_g, ml_norm_g, w_pa, w_pb, w_o, g_mlp, w_up, w_down, g_ple, w_ple_gate,
           w_ple_proj, g_final):
    bsz, seq, _ = x.shape
    t = bsz * seq
    assert w_in.shape[0] == 1, "single-layer block only"
    lambda_init = 0.8 - 0.6 * math.exp(-0.3 * 0)
    x2 = x.reshape(t, D_MODEL)
    row = lambda v: v.reshape(1, -1).astype(F32)

    w = w_in[0]
    c_mq, c_mv, c_mo, c_mi, c_ga = 3072, 5120, 6144, 7168, 7176
    w_main = jnp.concatenate(
        [w[:, :c_mq], w[:, c_mv:c_mo], w[:, c_ga:], w[:, c_mo:c_mi], w[:, c_mq:c_mv]],
        axis=1).astype(BF16)
    w_if = jnp.pad(w[:, c_mi:c_ga], ((0, 0), (0, LANES - 2 * ML_HEADS))).astype(BF16)
    bias_if = jnp.concatenate([b_i[0], b_f[0]]).astype(F32)
    lam4 = jnp.stack([lam_q1[0], lam_k1[0], lam_q2[0], lam_k2[0]]).astype(F32)

    zb, zqk, zif = _in_proj(x2, row(g_mix[0]), w_main, w_if)
    ya = _attention(zb, lam4, row(da_sub_g[0]), bsz=bsz, seq=seq, lambda_init=lambda_init)
    yb = _mlstm(zb, zqk, zif, bias_if, conv_w[0].astype(F32), row(conv_b[0]),
                row(ml_norm_g[0]), bsz=bsz, seq=seq)
    x2 = _merge(ya, yb, zb, x2, w_pa[0].astype(BF16), w_pb[0].astype(BF16), w_o[0].astype(BF16))
    x2 = _mlp(x2, row(g_mlp[0]), w_up[0].astype(BF16), w_down[0].astype(BF16))
    x2 = _ple(x2, p[0].reshape(t, PLE_DIM), row(g_ple[0]), w_ple_gate[0].astype(BF16),
              w_ple_proj[0].astype(BF16), row(g_final))
    return x2.reshape(bsz, seq, D_MODEL)
```

```python
import functools
import math

import jax
import jax.numpy as jnp
from jax import lax
from jax.experimental import pallas as pl
from jax.experimental.pallas import tpu as pltpu

F32 = jnp.float32
BF16 = jnp.bfloat16

D_MODEL = 2048
CHUNK = 64
EPS = 1e-6
DA_HEADS = 8
DA_V_DIM = 128
DA_QK_DIM = 64
DA_WIDTH = DA_HEADS * DA_V_DIM
ML_HEADS = 4
ML_HEAD_DIM = 256
ML_WIDTH = ML_HEADS * ML_HEAD_DIM
CONV_K = 4
D_FF = 4 * D_MODEL
PLE_DIM = 256
NEG_INF = -1e30

LANES = 128
SUBLANES = 8
VMEM_LIMIT_BYTES = 56 * 1024 * 1024

ZB_AQ, ZB_AK, ZB_AV, ZB_MV, ZB_GA, ZB_GB, ZB_MO = 0, 1024, 2048, 3072, 4096, 6144, 8192
ZB_COLS = 9216
ZQK_COLS = 2 * ML_WIDTH


def _rms(x, g):
    ms = jnp.mean(x * x, axis=-1, keepdims=True)
    return (x * lax.rsqrt(ms + EPS)) * g


def _dot(a, b):
    return jnp.dot(a, b, preferred_element_type=F32)


def _dot_nt(a, b):
    return lax.dot_general(a, b, (((1,), (1,)), ((), ())), preferred_element_type=F32)


def _dot_tn(a, b):
    return lax.dot_general(a, b, (((0,), (0,)), ((), ())), preferred_element_type=F32)


def _params(*sem):
    return pltpu.CompilerParams(dimension_semantics=sem, vmem_limit_bytes=VMEM_LIMIT_BYTES)


def _in_proj_kernel(x_ref, g_ref, w_ref, wif_ref, zb_ref, zqk_ref, zif_ref, h_ref, *, nb_bf16):
    j = pl.program_id(1)

    @pl.when(j == 0)
    def _():
        hb = _rms(x_ref[...], g_ref[...]).astype(BF16)
        h_ref[...] = hb
        zif_ref[...] = _dot(hb, wif_ref[...])

    @pl.when(j < nb_bf16)
    def _():
        zb_ref[...] = _dot(h_ref[...], w_ref[...]).astype(BF16)

    @pl.when(j >= nb_bf16)
    def _():
        zqk_ref[...] = _dot(h_ref[...], w_ref[...])


def _in_proj(x2, g_mix, w_main, w_if, *, tm=512, tn=1024):
    t = x2.shape[0]
    nb_bf16 = ZB_COLS // tn
    nb = w_main.shape[1] // tn
    return pl.pallas_call(
        functools.partial(_in_proj_kernel, nb_bf16=nb_bf16),
        grid=(t // tm, nb),
        in_specs=[
            pl.BlockSpec((tm, D_MODEL), lambda i, j: (i, 0)),
            pl.BlockSpec((1, D_MODEL), lambda i, j: (0, 0)),
            pl.BlockSpec((D_MODEL, tn), lambda i, j: (0, j)),
            pl.BlockSpec((D_MODEL, LANES), lambda i, j: (0, 0)),
        ],
        out_specs=[
            pl.BlockSpec((tm, tn), lambda i, j: (i, jnp.minimum(j, nb_bf16 - 1))),
            pl.BlockSpec((tm, tn), lambda i, j: (i, jnp.maximum(j - nb_bf16, 0))),
            pl.BlockSpec((tm, LANES), lambda i, j: (i, 0)),
        ],
        out_shape=[
            jax.ShapeDtypeStruct((t, ZB_COLS), BF16),
            jax.ShapeDtypeStruct((t, ZQK_COLS), F32),
            jax.ShapeDtypeStruct((t, LANES), F32),
        ],
        scratch_shapes=[pltpu.VMEM((tm, D_MODEL), BF16)],
        compiler_params=_params("parallel", "arbitrary"),
        name="in_proj",
    )(x2, g_mix, w_main, w_if)


def _attn_kernel(lam_ref, q_ref, k_ref, v_ref, g_ref, o_ref, vt_ref, s0_ref, s1_ref, acc_ref,
                 *, tq, lambda_init):
    qi = pl.program_id(2)
    nkb = vt_ref.shape[0]

    @pl.when(qi == 0)
    def _():
        for kb in range(nkb):
            vt_ref[kb] = v_ref[kb * tq:(kb + 1) * tq, :].astype(F32).T.astype(BF16)

    q = q_ref[...] * jnp.asarray(DA_QK_DIM ** -0.5, BF16)
    lane = lax.broadcasted_iota(jnp.int32, q.shape, 1)
    zero = jnp.zeros_like(q)
    q_maps = (jnp.where(lane < DA_QK_DIM, q, zero), jnp.where(lane >= DA_QK_DIM, q, zero))
    acc_ref[...] = jnp.zeros(acc_ref.shape, F32)

    def scores(kb, s_ref):
        k = k_ref[pl.ds(pl.multiple_of(kb * tq, tq), tq), :]
        for mp in range(2):
            s_ref[mp] = _dot_nt(k, q_maps[mp])

    def softmax_pv(kb, s_ref, carry, masked=False):
        new = []
        for mp in range(2):
            m_prev, l_prev = carry[mp]
            s = s_ref[mp]
            if masked:
                key = lax.broadcasted_iota(jnp.int32, (tq, tq), 0)
                qry = lax.broadcasted_iota(jnp.int32, (tq, tq), 1)
                s = jnp.where((key // CHUNK) <= (qry // CHUNK), s, NEG_INF)
            blk_max = jnp.max(s.reshape(tq // SUBLANES, SUBLANES, tq).max(axis=0), axis=0, keepdims=True)
            m_next = jnp.maximum(m_prev, blk_max)
            alpha = jnp.exp(m_prev - m_next)
            p = jnp.exp(s - m_next)
            l_next = alpha * l_prev + p.reshape(tq // SUBLANES, SUBLANES, tq).sum(axis=0)
            acc_ref[mp] = alpha * acc_ref[mp] + _dot(vt_ref[kb], p.astype(BF16))
            new.append((m_next, l_next))
        return tuple(new)

    def pair(j, carry):
        kb = 2 * j
        scores(kb + 1, s1_ref)
        carry = softmax_pv(kb, s0_ref, carry)
        scores(kb + 2, s0_ref)
        return softmax_pv(kb + 1, s1_ref, carry)

    def finish(carry):
        (_, l1), (_, l2) = carry
        lq = lam_ref[...]
        t1 = jnp.sum(lq[0:1, :] * lq[1:2, :], axis=-1, keepdims=True)
        t2 = jnp.sum(lq[2:3, :] * lq[3:4, :], axis=-1, keepdims=True)
        lam = jnp.exp(t1) - jnp.exp(t2) + lambda_init
        l1 = jnp.sum(l1, axis=0, keepdims=True)
        l2 = jnp.sum(l2, axis=0, keepdims=True)
        o = acc_ref[0] / l1 - lam * (acc_ref[1] / l2)
        ms = jnp.mean(o * o, axis=0, keepdims=True)
        y = ((o * lax.rsqrt(ms + EPS)) * g_ref[...]) * (1.0 - lambda_init)
        o_ref[...] = y.T.astype(o_ref.dtype)

    init = (jnp.full((1, tq), NEG_INF, F32), jnp.zeros((SUBLANES, tq), F32))
    scores(0, s0_ref)
    carry = lax.fori_loop(0, qi // 2, pair, (init, init))

    @pl.when(qi % 2 == 0)
    def _():
        finish(softmax_pv(qi, s0_ref, carry, masked=True))

    @pl.when(qi % 2 == 1)
    def _():
        scores(qi, s1_ref)
        c = softmax_pv(qi - 1, s0_ref, carry)
        finish(softmax_pv(qi, s1_ref, c, masked=True))


def _attention(zb, lam4, sub_g_col, *, bsz, seq, lambda_init, tq=256):
    nq = seq // tq
    kq, kk, kv = ZB_AQ // DA_V_DIM, ZB_AK // DA_V_DIM, ZB_AV // DA_V_DIM
    return pl.pallas_call(
        functools.partial(_attn_kernel, tq=tq, lambda_init=lambda_init),
        grid=(bsz, DA_HEADS, nq),
        in_specs=[
            pl.BlockSpec((4, DA_QK_DIM), lambda b, h, i: (0, 0)),
            pl.BlockSpec((tq, DA_V_DIM), lambda b, h, i: (b * nq + i, kq + h)),
            pl.BlockSpec((seq, DA_V_DIM), lambda b, h, i: (b, kk + h)),
            pl.BlockSpec((seq, DA_V_DIM), lambda b, h, i: (b, kv + h)),
            pl.BlockSpec((DA_V_DIM, 1), lambda b, h, i: (0, 0)),
        ],
        out_specs=pl.BlockSpec((tq, DA_V_DIM), lambda b, h, i: (b * nq + i, h)),
        out_shape=jax.ShapeDtypeStruct((bsz * seq, DA_WIDTH), BF16),
        scratch_shapes=[
            pltpu.VMEM((nq, DA_V_DIM, tq), BF16),
            pltpu.VMEM((2, tq, tq), F32),
            pltpu.VMEM((2, tq, tq), F32),
            pltpu.VMEM((2, DA_V_DIM, tq), F32),
        ],
        compiler_params=_params("parallel", "parallel", "arbitrary"),
        name="diff_attn",
    )(lam4, zb, zb, zb, sub_g_col)


def _log_sigmoid(x):
    return jnp.minimum(x, 0.0) - jnp.log1p(jnp.exp(-jnp.abs(x)))


def _mlstm_kernel(bias_ref, qpre_ref, kpre_ref, v_ref, og_ref, zif_ref,
                  cwq_ref, cwk_ref, cbq_ref, cbk_ref, g_ref, y_ref, c_ref, *, seq):
    L = CHUNK
    d = ML_HEAD_DIM
    h = pl.program_id(1)
    b_i = bias_ref[h]
    b_f = bias_ref[ML_HEADS + h]
    c_ref[...] = jnp.zeros(c_ref.shape, F32)

    lane = lax.broadcasted_iota(jnp.int32, (L, LANES), 1)
    sel_i = lane == h
    sel_f = lane == h + ML_HEADS
    row = lax.broadcasted_iota(jnp.int32, (L, L), 0)
    col = lax.broadcasted_iota(jnp.int32, (L, L), 1)
    causal = col <= row
    eye = col == row
    qscale = d ** -0.5

    def conv_silu(pre_ref, w_ref, b_ref, c, t0):
        cur = pre_ref[pl.ds(t0, L), :]
        p0 = pl.multiple_of(jnp.maximum(t0 - SUBLANES, 0), SUBLANES)
        prev = jnp.where(c > 0, pre_ref[pl.ds(p0, SUBLANES), :], 0.0)
        xw = jnp.concatenate([prev, cur], axis=0)
        y = b_ref[...]
        for j in range(CONV_K):
            off = SUBLANES - (CONV_K - 1) + j
            y = y + w_ref[j:j + 1, :] * xw[off:off + L, :]
        return y * jax.nn.sigmoid(y)

    def chunk(c, carry):
        m_prev, n_prev = carry
        t0 = pl.multiple_of(c * L, L)
        z = zif_ref[pl.ds(t0, L), :]
        i_col = jnp.sum(jnp.where(sel_i, z, 0.0), axis=1, keepdims=True) + b_i
        f_col = _log_sigmoid(jnp.sum(jnp.where(sel_f, z, 0.0), axis=1, keepdims=True) + b_f)
        f_mat = jnp.broadcast_to(f_col, (L, L))
        i_mat = jnp.broadcast_to(i_col, (L, L))
        b_row = jnp.sum(jnp.where(row <= col, f_mat, 0.0), axis=0, keepdims=True)
        i_row = jnp.sum(jnp.where(eye, i_mat, 0.0), axis=0, keepdims=True)
        b_col = jnp.sum(jnp.where(eye, jnp.broadcast_to(b_row, (L, L)), 0.0), axis=1, keepdims=True)

        dlog = jnp.where(causal, b_col - b_row + i_row, NEG_INF)
        inter_log = b_col + m_prev
        m_t = jnp.maximum(inter_log, jnp.max(dlog, axis=1, keepdims=True))
        dmat = jnp.exp(dlog - m_t)
        inter_w = jnp.exp(inter_log - m_t)

        q = conv_silu(qpre_ref, cwq_ref, cbq_ref, c, t0)
        k = conv_silu(kpre_ref, cwk_ref, cbk_ref, c, t0)
        qb = q.astype(BF16)
        kb = k.astype(BF16)
        v = v_ref[pl.ds(t0, L), :]

        s = (_dot_nt(qb, kb) * qscale) * dmat
        inter = _dot(qb, c_ref[...].astype(BF16)) * qscale
        num = _dot(s.astype(BF16), v) + inter_w * inter
        qn = jnp.sum(q * n_prev, axis=1, keepdims=True) * qscale
        den = jnp.sum(s, axis=1, keepdims=True) + inter_w * qn
        hh = num / jnp.maximum(jnp.abs(den), jnp.exp(-m_t))

        hn = _rms(hh, g_ref[...])
        og = og_ref[pl.ds(t0, L), :].astype(F32)
        y_ref[pl.ds(t0, L), :] = (jax.nn.sigmoid(og) * hn).astype(y_ref.dtype)

        b_last = b_col[L - 1:L, :]
        upd = b_last - b_col + i_col
        m_new = jnp.maximum(b_last + m_prev, jnp.max(upd, axis=0, keepdims=True))
        w_col = jnp.exp(upd - m_new)
        decay = jnp.exp(b_last + m_prev - m_new)
        kw = k * w_col
        c_ref[...] = decay * c_ref[...] + _dot_tn(kw.astype(BF16), v)
        n_new = decay * n_prev + jnp.sum(kw, axis=0, keepdims=True)
        return m_new, n_new

    lax.fori_loop(0, seq // L, chunk, (jnp.zeros((1, 1), F32), jnp.zeros((1, d), F32)))


def _mlstm(zb, zqk, zif, bias_if, conv_w, conv_b, norm_g, *, bsz, seq):
    d = ML_HEAD_DIM
    kmv, kmo = ZB_MV // d, ZB_MO // d
    return pl.pallas_call(
        functools.partial(_mlstm_kernel, seq=seq),
        grid=(bsz, ML_HEADS),
        in_specs=[
            pl.BlockSpec(memory_space=pltpu.SMEM),
            pl.BlockSpec((seq, d), lambda b, h: (b, h)),
            pl.BlockSpec((seq, d), lambda b, h: (b, ML_HEADS + h)),
            pl.BlockSpec((seq, d), lambda b, h: (b, kmv + h)),
            pl.BlockSpec((seq, d), lambda b, h: (b, kmo + h)),
            pl.BlockSpec((seq, LANES), lambda b, h: (b, 0)),
            pl.BlockSpec((CONV_K, d), lambda b, h: (0, h)),
            pl.BlockSpec((CONV_K, d), lambda b, h: (0, ML_HEADS + h)),
            pl.BlockSpec((1, d), lambda b, h: (0, h)),
            pl.BlockSpec((1, d), lambda b, h: (0, ML_HEADS + h)),
            pl.BlockSpec((1, d), lambda b, h: (0, h)),
        ],
        out_specs=pl.BlockSpec((seq, d), lambda b, h: (b, h)),
        out_shape=jax.ShapeDtypeStruct((bsz * seq, ML_WIDTH), BF16),
        scratch_shapes=[pltpu.VMEM((d, d), F32)],
        compiler_params=_params("parallel", "parallel"),
        name="mlstm",
    )(bias_if, zqk, zqk, zb, zb, zif, conv_w, conv_w, conv_b, conv_b, norm_g)


def _merge_kernel(ya_ref, yb_ref, ga_ref, gb_ref, x_ref, wpa_ref, wpb_ref, wo_ref, o_ref):
    pa = _dot(ya_ref[...], wpa_ref[...])
    pb = _dot(yb_ref[...], wpb_ref[...])
    merged = (jax.nn.sigmoid(ga_ref[...].astype(F32)) * pa
              + jax.nn.sigmoid(gb_ref[...].astype(F32)) * pb)
    o_ref[...] = x_ref[...] + _dot(merged.astype(BF16), wo_ref[...])


def _resident(shape):
    return pl.BlockSpec(shape, lambda *_: (0,) * len(shape), pipeline_mode=pl.Buffered(1))


def _merge(ya, yb, zb, x2, w_pa, w_pb, w_o, *, tm=256):
    t = x2.shape[0]
    kga, kgb = ZB_GA // D_MODEL, ZB_GB // D_MODEL
    return pl.pallas_call(
        _merge_kernel,
        grid=(t // tm,),
        in_specs=[
            pl.BlockSpec((tm, DA_WIDTH), lambda i: (i, 0)),
            pl.BlockSpec((tm, ML_WIDTH), lambda i: (i, 0)),
            pl.BlockSpec((tm, D_MODEL), lambda i: (i, kga)),
            pl.BlockSpec((tm, D_MODEL), lambda i: (i, kgb)),
            pl.BlockSpec((tm, D_MODEL), lambda i: (i, 0)),
            _resident((DA_WIDTH, D_MODEL)),
            _resident((ML_WIDTH, D_MODEL)),
            _resident((D_MODEL, D_MODEL)),
        ],
        out_specs=pl.BlockSpec((tm, D_MODEL), lambda i: (i, 0)),
        out_shape=jax.ShapeDtypeStruct((t, D_MODEL), F32),
        compiler_params=_params("parallel"),
        name="merge",
    )(ya, yb, zb, zb, x2, w_pa, w_pb, w_o)


def _mlp_kernel(x_ref, g_ref, wup_ref, wdn_ref, o_ref, h_ref):
    f = pl.program_id(1)

    @pl.when(f == 0)
    def _():
        x = x_ref[...]
        h_ref[...] = _rms(x, g_ref[...]).astype(BF16)
        o_ref[...] = x

    u = _dot(h_ref[...], wup_ref[...])
    a = jnp.square(jnp.maximum(u, 0.0)).astype(BF16)
    o_ref[...] += _dot(a, wdn_ref[...])


def _mlp(x1, g_mlp, w_up, w_down, *, tm=512, tf=1024):
    t = x1.shape[0]
    return pl.pallas_call(
        _mlp_kernel,
        grid=(t // tm, D_FF // tf),
        in_specs=[
            pl.BlockSpec((tm, D_MODEL), lambda i, f: (i, 0)),
            pl.BlockSpec((1, D_MODEL), lambda i, f: (0, 0)),
            pl.BlockSpec((D_MODEL, tf), lambda i, f: (0, f)),
            pl.BlockSpec((tf, D_MODEL), lambda i, f: (f, 0)),
        ],
        out_specs=pl.BlockSpec((tm, D_MODEL), lambda i, f: (i, 0)),
        out_shape=jax.ShapeDtypeStruct((t, D_MODEL), F32),
        scratch_shapes=[pltpu.VMEM((tm, D_MODEL), BF16)],
        compiler_params=_params("parallel", "arbitrary"),
        name="mlp",
    )(x1, g_mlp, w_up, w_down)


def _ple_kernel(x_ref, p_ref, gp_ref, wg_ref, wp_ref, gf_ref, o_ref):
    x = x_ref[...]
    gate = jax.nn.sigmoid(_dot(_rms(x, gp_ref[...]).astype(BF16), wg_ref[...]))
    proj = _dot(p_ref[...].astype(BF16), wp_ref[...])
    o_ref[...] = _rms(x + gate * proj, gf_ref[...])


def _ple(x2, p2, g_ple, w_gate, w_proj, g_final, *, tm=256):
    t = x2.shape[0]
    return pl.pallas_call(
        _ple_kernel,
        grid=(t // tm,),
        in_specs=[
            pl.BlockSpec((tm, D_MODEL), lambda i: (i, 0)),
            pl.BlockSpec((tm, PLE_DIM), lambda i: (i, 0)),
            pl.BlockSpec((1, D_MODEL), lambda i: (0, 0)),
            _resident((D_MODEL, D_MODEL)),
            _resident((PLE_DIM, D_MODEL)),
            pl.BlockSpec((1, D_MODEL), lambda i: (0, 0)),
        ],
        out_specs=pl.BlockSpec((tm, D_MODEL), lambda i: (i, 0)),
        out_shape=jax.ShapeDtypeStruct((t, D_MODEL), F32),
        compiler_params=_params("parallel"),
        name="ple_final",
    )(x2, p2, g_ple, w_gate, w_proj, g_final)


def kernel(x, p, g_mix, w_in, conv_w, conv_b, b_i, b_f, lam_q1, lam_k1, lam_q2, lam_k2,
           da_sub_g, ml_norm_g, w_pa, w_pb, w_o, g_mlp, w_up, w_down, g_ple, w_ple_gate,
           w_ple_proj, g_final):
    bsz, seq, _ = x.shape
    t = bsz * seq
    assert w_in.shape[0] == 1, "single-layer block only"
    lambda_init = 0.8 - 0.6 * math.exp(-0.3 * 0)
    x2 = x.reshape(t, D_MODEL)
    row = lambda v: v.reshape(1, -1).astype(F32)

    w = w_in[0]
    c_mq, c_mv, c_mo, c_mi, c_ga = 3072, 5120, 6144, 7168, 7176
    w_main = jnp.concatenate(
        [w[:, :c_mq], w[:, c_mv:c_mo], w[:, c_ga:], w[:, c_mo:c_mi], w[:, c_mq:c_mv]],
        axis=1).astype(BF16)
    w_if = jnp.pad(w[:, c_mi:c_ga], ((0, 0), (0, LANES - 2 * ML_HEADS))).astype(BF16)
    bias_if = jnp.concatenate([b_i[0], b_f[0]]).astype(F32)
    lam4 = jnp.stack([lam_q1[0], lam_k1[0], lam_q2[0], lam_k2[0]]).astype(F32)

    zb, zqk, zif = _in_proj(x2, row(g_mix[0]), w_main, w_if)
    ya = _attention(zb, lam4, da_sub_g[0].reshape(DA_V_DIM, 1).astype(F32), bsz=bsz, seq=seq,
                    lambda_init=lambda_init)
    yb = _mlstm(zb, zqk, zif, bias_if, conv_w[0].astype(F32), row(conv_b[0]),
                row(ml_norm_g[0]), bsz=bsz, seq=seq)
    x2 = _merge(ya, yb, zb, x2, w_pa[0].astype(BF16), w_pb[0].astype(BF16), w_o[0].astype(BF16))
    x2 = _mlp(x2, row(g_mlp[0]), w_up[0].astype(BF16), w_down[0].astype(BF16))
    x2 = _ple(x2, p[0].reshape(t, PLE_DIM), row(g_ple[0]), w_ple_gate[0].astype(BF16),
              w_ple_proj[0].astype(BF16), row(g_final))
    return x2.reshape(bsz, seq, D_MODEL)
```

```python
import functools
import math

import jax
import jax.numpy as jnp
from jax import lax
from jax.experimental import pallas as pl
from jax.experimental.pallas import tpu as pltpu

F32 = jnp.float32
BF16 = jnp.bfloat16

D_MODEL = 2048
CHUNK = 64
EPS = 1e-6
DA_HEADS = 8
DA_V_DIM = 128
DA_QK_DIM = 64
DA_WIDTH = DA_HEADS * DA_V_DIM
ML_HEADS = 4
ML_HEAD_DIM = 256
ML_WIDTH = ML_HEADS * ML_HEAD_DIM
CONV_K = 4
D_FF = 4 * D_MODEL
PLE_DIM = 256
NEG_INF = -1e30

LANES = 128
SUBLANES = 8
VMEM_LIMIT_BYTES = 56 * 1024 * 1024

ZB_AQ, ZB_AK, ZB_AV, ZB_MV, ZB_GA, ZB_GB, ZB_MO = 0, 1024, 2048, 3072, 4096, 6144, 8192
ZB_COLS = 9216
ZQK_COLS = 2 * ML_WIDTH
ATTN_Q_SCALE = DA_QK_DIM ** -0.5 * math.log2(math.e)


def _rms(x, g):
    ms = jnp.mean(x * x, axis=-1, keepdims=True)
    return (x * lax.rsqrt(ms + EPS)) * g


def _dot(a, b):
    return jnp.dot(a, b, preferred_element_type=F32)


def _dot_nt(a, b):
    return lax.dot_general(a, b, (((1,), (1,)), ((), ())), preferred_element_type=F32)


def _dot_tn(a, b):
    return lax.dot_general(a, b, (((0,), (0,)), ((), ())), preferred_element_type=F32)


def _params(*sem):
    return pltpu.CompilerParams(dimension_semantics=sem, vmem_limit_bytes=VMEM_LIMIT_BYTES)


def _in_proj_kernel(x_ref, g_ref, w_ref, wif_ref, cw_ref, cb_ref, zb_ref, zqk_ref, zif_ref,
                    h_ref, tail_ref, *, nb_bf16, tiles_per_seq):
    i = pl.program_id(0)
    j = pl.program_id(1)
    tm = x_ref.shape[0]

    @pl.when(j == 0)
    def _():
        hb = _rms(x_ref[...], g_ref[...]).astype(BF16)
        h_ref[...] = hb
        zif_ref[...] = _dot(hb, wif_ref[...])
        zb_ref[...] = (_dot(hb, w_ref[...]) * ATTN_Q_SCALE).astype(BF16)

    @pl.when((j > 0) & (j < nb_bf16))
    def _():
        zb_ref[...] = _dot(h_ref[...], w_ref[...]).astype(BF16)

    @pl.when(j >= nb_bf16)
    def _():
        jj = j - nb_bf16
        z = _dot(h_ref[...], w_ref[...])
        prev = jnp.where(i % tiles_per_seq == 0, 0.0, tail_ref[jj])
        tail_ref[jj] = z[tm - SUBLANES:, :]
        xw = jnp.concatenate([prev, z], axis=0)
        y = cb_ref[...]
        for tap in range(CONV_K):
            off = SUBLANES - (CONV_K - 1) + tap
            y = y + cw_ref[tap:tap + 1, :] * xw[off:off + tm, :]
        zqk_ref[...] = (y * jax.nn.sigmoid(y)).astype(BF16)


def _in_proj(x2, g_mix, w_main, w_if, conv_w, conv_b, *, seq, tm=512, tn=1024):
    t = x2.shape[0]
    nb_bf16 = ZB_COLS // tn
    nb = w_main.shape[1] // tn
    qk_blk = lambda i, j: (0, jnp.maximum(j - nb_bf16, 0))
    return pl.pallas_call(
        functools.partial(_in_proj_kernel, nb_bf16=nb_bf16, tiles_per_seq=seq // tm),
        grid=(t // tm, nb),
        in_specs=[
            pl.BlockSpec((tm, D_MODEL), lambda i, j: (i, 0)),
            pl.BlockSpec((1, D_MODEL), lambda i, j: (0, 0)),
            pl.BlockSpec((D_MODEL, tn), lambda i, j: (0, j)),
            pl.BlockSpec((D_MODEL, LANES), lambda i, j: (0, 0)),
            pl.BlockSpec((CONV_K, tn), qk_blk),
            pl.BlockSpec((1, tn), qk_blk),
        ],
        out_specs=[
            pl.BlockSpec((tm, tn), lambda i, j: (i, jnp.minimum(j, nb_bf16 - 1))),
            pl.BlockSpec((tm, tn), lambda i, j: (i, jnp.maximum(j - nb_bf16, 0))),
            pl.BlockSpec((tm, LANES), lambda i, j: (i, 0)),
        ],
        out_shape=[
            jax.ShapeDtypeStruct((t, ZB_COLS), BF16),
            jax.ShapeDtypeStruct((t, ZQK_COLS), BF16),
            jax.ShapeDtypeStruct((t, LANES), F32),
        ],
        scratch_shapes=[
            pltpu.VMEM((tm, D_MODEL), BF16),
            pltpu.VMEM((ZQK_COLS // tn, SUBLANES, tn), F32),
        ],
        compiler_params=_params("arbitrary", "arbitrary"),
        name="in_proj",
    )(x2, g_mix, w_main, w_if, conv_w, conv_b)


def _attn_kernel(lam_ref, q_ref, k_ref, v_ref, g_ref, o_ref, vt_ref, s0_ref, s1_ref, acc_ref,
                 *, tq, tk, lambda_init):
    seq = q_ref.shape[0]
    dv = DA_V_DIM

    for kb in range(seq // tk):
        vt_ref[kb, :dv, :] = v_ref[kb * tk:(kb + 1) * tk, :].astype(F32).T.astype(BF16)
        vt_ref[kb, dv:, :] = jnp.ones((vt_ref.shape[1] - dv, tk), BF16)

    lq = lam_ref[...]
    t1 = jnp.sum(lq[0:1, :] * lq[1:2, :], axis=-1, keepdims=True)
    t2 = jnp.sum(lq[2:3, :] * lq[3:4, :], axis=-1, keepdims=True)
    lam = jnp.exp(t1) - jnp.exp(t2) + lambda_init
    s_bufs = (s0_ref, s1_ref)

    for qi in range(seq // tq):
        q = q_ref[qi * tq:(qi + 1) * tq, :]
        lane = lax.broadcasted_iota(jnp.int32, q.shape, 1)
        zero = jnp.zeros_like(q)
        q_maps = (jnp.where(lane < DA_QK_DIM, q, zero), jnp.where(lane >= DA_QK_DIM, q, zero))
        n_kb = (qi + 1) * tq // tk
        n_full = qi * tq // tk

        def scores(kb):
            k = k_ref[kb * tk:(kb + 1) * tk, :]
            for mp in range(2):
                s_bufs[kb % 2][mp] = _dot_nt(k, q_maps[mp])

        m_run = [jnp.full((1, tq), NEG_INF, F32)] * 2
        scores(0)
        for kb in range(n_kb):
            if kb + 1 < n_kb:
                scores(kb + 1)
            for mp in range(2):
                s = s_bufs[kb % 2][mp]
                if kb >= n_full:
                    key = lax.broadcasted_iota(jnp.int32, (tk, tq), 0) + (kb - n_full) * tk
                    qry = lax.broadcasted_iota(jnp.int32, (tk, tq), 1)
                    s = jnp.where((key // CHUNK) <= (qry // CHUNK), s, NEG_INF)
                blk_max = jnp.max(s.reshape(tk // SUBLANES, SUBLANES, tq).max(axis=0),
                                  axis=0, keepdims=True)
                m_next = jnp.maximum(m_run[mp], blk_max)
                p = jnp.exp2(s - m_next).astype(BF16)
                pv = _dot(vt_ref[kb], p)
                if kb == 0:
                    acc_ref[mp] = pv
                else:
                    acc_ref[mp] = jnp.exp2(m_run[mp] - m_next) * acc_ref[mp] + pv
                m_run[mp] = m_next

        a1 = acc_ref[0]
        a2 = acc_ref[1]
        o = a1[:dv] / a1[dv:dv + 1] - lam * (a2[:dv] / a2[dv:dv + 1])
        ms = jnp.mean(o * o, axis=0, keepdims=True)
        y = ((o * lax.rsqrt(ms + EPS)) * g_ref[...]) * (1.0 - lambda_init)
        o_ref[qi * tq:(qi + 1) * tq, :] = y.T.astype(o_ref.dtype)


def _attention(zb, lam4, sub_g_col, *, bsz, seq, lambda_init, tq=512, tk=256):
    kq, kk, kv = ZB_AQ // DA_V_DIM, ZB_AK // DA_V_DIM, ZB_AV // DA_V_DIM
    vt_rows = DA_V_DIM + 16
    return pl.pallas_call(
        functools.partial(_attn_kernel, tq=tq, tk=tk, lambda_init=lambda_init),
        grid=(bsz, DA_HEADS),
        in_specs=[
            pl.BlockSpec((4, DA_QK_DIM), lambda b, h: (0, 0)),
            pl.BlockSpec((seq, DA_V_DIM), lambda b, h: (b, kq + h)),
            pl.BlockSpec((seq, DA_V_DIM), lambda b, h: (b, kk + h)),
            pl.BlockSpec((seq, DA_V_DIM), lambda b, h: (b, kv + h)),
            pl.BlockSpec((DA_V_DIM, 1), lambda b, h: (0, 0)),
        ],
        out_specs=pl.BlockSpec((seq, DA_V_DIM), lambda b, h: (b, h)),
        out_shape=jax.ShapeDtypeStruct((bsz * seq, DA_WIDTH), BF16),
        scratch_shapes=[
            pltpu.VMEM((seq // tk, vt_rows, tk), BF16),
            pltpu.VMEM((2, tk, tq), F32),
            pltpu.VMEM((2, tk, tq), F32),
            pltpu.VMEM((2, vt_rows, tq), F32),
        ],
        compiler_params=_params("parallel", "parallel"),
        name="diff_attn",
    )(lam4, zb, zb, zb, sub_g_col)


def _log_sigmoid(x):
    return jnp.minimum(x, 0.0) - jnp.log1p(jnp.exp(-jnp.abs(x)))


def _mlstm_kernel(bias_ref, q_ref, k_ref, v_ref, og_ref, zif_ref, g_ref, y_ref, c_ref, *, seq):
    L = CHUNK
    d = ML_HEAD_DIM
    h = pl.program_id(1)
    b_i = bias_ref[h]
    b_f = bias_ref[ML_HEADS + h]
    c_ref[...] = jnp.zeros(c_ref.shape, F32)

    lane = lax.broadcasted_iota(jnp.int32, (L, LANES), 1)
    sel_i = lane == h
    sel_f = lane == h + ML_HEADS
    row = lax.broadcasted_iota(jnp.int32, (L, L), 0)
    col = lax.broadcasted_iota(jnp.int32, (L, L), 1)
    causal = col <= row
    eye = col == row
    qscale = d ** -0.5

    def chunk(c, carry):
        m_prev, n_prev = carry
        t0 = pl.multiple_of(c * L, L)
        z = zif_ref[pl.ds(t0, L), :]
        i_col = jnp.sum(jnp.where(sel_i, z, 0.0), axis=1, keepdims=True) + b_i
        f_col = _log_sigmoid(jnp.sum(jnp.where(sel_f, z, 0.0), axis=1, keepdims=True) + b_f)
        f_mat = jnp.broadcast_to(f_col, (L, L))
        i_mat = jnp.broadcast_to(i_col, (L, L))
        b_row = jnp.sum(jnp.where(row <= col, f_mat, 0.0), axis=0, keepdims=True)
        i_row = jnp.sum(jnp.where(eye, i_mat, 0.0), axis=0, keepdims=True)
        b_col = jnp.sum(jnp.where(eye, jnp.broadcast_to(b_row, (L, L)), 0.0), axis=1, keepdims=True)

        dlog = jnp.where(causal, b_col - b_row + i_row, NEG_INF)
        inter_log = b_col + m_prev
        m_t = jnp.maximum(inter_log, jnp.max(dlog, axis=1, keepdims=True))
        dmat = jnp.exp(dlog - m_t)
        inter_w = jnp.exp(inter_log - m_t)

        qb = q_ref[pl.ds(t0, L), :]
        kb = k_ref[pl.ds(t0, L), :]
        q = qb.astype(F32)
        k = kb.astype(F32)
        v = v_ref[pl.ds(t0, L), :]

        s = (_dot_nt(qb, kb) * qscale) * dmat
        inter = _dot(qb, c_ref[...].astype(BF16)) * qscale
        num = _dot(s.astype(BF16), v) + inter_w * inter
        qn = jnp.sum(q * n_prev, axis=1, keepdims=True) * qscale
        den = jnp.sum(s, axis=1, keepdims=True) + inter_w * qn
        hh = num / jnp.maximum(jnp.abs(den), jnp.exp(-m_t))

        hn = _rms(hh, g_ref[...])
        og = og_ref[pl.ds(t0, L), :].astype(F32)
        y_ref[pl.ds(t0, L), :] = (jax.nn.sigmoid(og) * hn).astype(y_ref.dtype)

        b_last = b_col[L - 1:L, :]
        upd = b_last - b_col + i_col
        m_new = jnp.maximum(b_last + m_prev, jnp.max(upd, axis=0, keepdims=True))
        w_col = jnp.exp(upd - m_new)
        decay = jnp.exp(b_last + m_prev - m_new)
        kw = k * w_col
        c_ref[...] = decay * c_ref[...] + _dot_tn(kw.astype(BF16), v)
        n_new = decay * n_prev + jnp.sum(kw, axis=0, keepdims=True)
        return m_new, n_new

    lax.fori_loop(0, seq // L, chunk, (jnp.zeros((1, 1), F32), jnp.zeros((1, d), F32)), unroll=4)


def _mlstm(zb, zqk, zif, bias_if, norm_g, *, bsz, seq):
    d = ML_HEAD_DIM
    kmv, kmo = ZB_MV // d, ZB_MO // d
    return pl.pallas_call(
        functools.partial(_mlstm_kernel, seq=seq),
        grid=(bsz, ML_HEADS),
        in_specs=[
            pl.BlockSpec(memory_space=pltpu.SMEM),
            pl.BlockSpec((seq, d), lambda b, h: (b, h)),
            pl.BlockSpec((seq, d), lambda b, h: (b, ML_HEADS + h)),
            pl.BlockSpec((seq, d), lambda b, h: (b, kmv + h)),
            pl.BlockSpec((seq, d), lambda b, h: (b, kmo + h)),
            pl.BlockSpec((seq, LANES), lambda b, h: (b, 0)),
            pl.BlockSpec((1, d), lambda b, h: (0, h)),
        ],
        out_specs=pl.BlockSpec((seq, d), lambda b, h: (b, h)),
        out_shape=jax.ShapeDtypeStruct((bsz * seq, ML_WIDTH), BF16),
        scratch_shapes=[pltpu.VMEM((d, d), F32)],
        compiler_params=_params("parallel", "parallel"),
        name="mlstm",
    )(bias_if, zqk, zqk, zb, zb, zif, norm_g)


def _merge_kernel(ya_ref, yb_ref, ga_ref, gb_ref, x_ref, wpa_ref, wpb_ref, wo_ref, o_ref):
    pa = _dot(ya_ref[...], wpa_ref[...])
    pb = _dot(yb_ref[...], wpb_ref[...])
    merged = (jax.nn.sigmoid(ga_ref[...].astype(F32)) * pa
              + jax.nn.sigmoid(gb_ref[...].astype(F32)) * pb)
    o_ref[...] = x_ref[...] + _dot(merged.astype(BF16), wo_ref[...])


def _resident(shape):
    return pl.BlockSpec(shape, lambda *_: (0,) * len(shape), pipeline_mode=pl.Buffered(1))


def _merge(ya, yb, zb, x2, w_pa, w_pb, w_o, *, tm=256):
    t = x2.shape[0]
    kga, kgb = ZB_GA // D_MODEL, ZB_GB // D_MODEL
    return pl.pallas_call(
        _merge_kernel,
        grid=(t // tm,),
        in_specs=[
            pl.BlockSpec((tm, DA_WIDTH), lambda i: (i, 0)),
            pl.BlockSpec((tm, ML_WIDTH), lambda i: (i, 0)),
            pl.BlockSpec((tm, D_MODEL), lambda i: (i, kga)),
            pl.BlockSpec((tm, D_MODEL), lambda i: (i, kgb)),
            pl.BlockSpec((tm, D_MODEL), lambda i: (i, 0)),
            _resident((DA_WIDTH, D_MODEL)),
            _resident((ML_WIDTH, D_MODEL)),
            _resident((D_MODEL, D_MODEL)),
        ],
        out_specs=pl.BlockSpec((tm, D_MODEL), lambda i: (i, 0)),
        out_shape=jax.ShapeDtypeStruct((t, D_MODEL), F32),
        compiler_params=_params("parallel"),
        name="merge",
    )(ya, yb, zb, zb, x2, w_pa, w_pb, w_o)


def _mlp_kernel(x_ref, g_ref, wup_ref, wdn_ref, o_ref, h_ref):
    f = pl.program_id(1)

    @pl.when(f == 0)
    def _():
        x = x_ref[...]
        h_ref[...] = _rms(x, g_ref[...]).astype(BF16)
        o_ref[...] = x

    u = _dot(h_ref[...], wup_ref[...])
    a = jnp.square(jnp.maximum(u, 0.0)).astype(BF16)
    o_ref[...] += _dot(a, wdn_ref[...])


def _mlp(x1, g_mlp, w_up, w_down, *, tm=512, tf=1024):
    t = x1.shape[0]
    return pl.pallas_call(
        _mlp_kernel,
        grid=(t // tm, D_FF // tf),
        in_specs=[
            pl.BlockSpec((tm, D_MODEL), lambda i, f: (i, 0)),
            pl.BlockSpec((1, D_MODEL), lambda i, f: (0, 0)),
            pl.BlockSpec((D_MODEL, tf), lambda i, f: (0, f)),
            pl.BlockSpec((tf, D_MODEL), lambda i, f: (f, 0)),
        ],
        out_specs=pl.BlockSpec((tm, D_MODEL), lambda i, f: (i, 0)),
        out_shape=jax.ShapeDtypeStruct((t, D_MODEL), F32),
        scratch_shapes=[pltpu.VMEM((tm, D_MODEL), BF16)],
        compiler_params=_params("parallel", "arbitrary"),
        name="mlp",
    )(x1, g_mlp, w_up, w_down)


def _ple_kernel(x_ref, p_ref, gp_ref, wg_ref, wp_ref, gf_ref, o_ref):
    x = x_ref[...]
    gate = jax.nn.sigmoid(_dot(_rms(x, gp_ref[...]).astype(BF16), wg_ref[...]))
    proj = _dot(p_ref[...].astype(BF16), wp_ref[...])
    o_ref[...] = _rms(x + gate * proj, gf_ref[...])


def _ple(x2, p2, g_ple, w_gate, w_proj, g_final, *, tm=256):
    t = x2.shape[0]
    return pl.pallas_call(
        _ple_kernel,
        grid=(t // tm,),
        in_specs=[
            pl.BlockSpec((tm, D_MODEL), lambda i: (i, 0)),
            pl.BlockSpec((tm, PLE_DIM), lambda i: (i, 0)),
            pl.BlockSpec((1, D_MODEL), lambda i: (0, 0)),
            _resident((D_MODEL, D_MODEL)),
            _resident((PLE_DIM, D_MODEL)),
            pl.BlockSpec((1, D_MODEL), lambda i: (0, 0)),
        ],
        out_specs=pl.BlockSpec((tm, D_MODEL), lambda i: (i, 0)),
        out_shape=jax.ShapeDtypeStruct((t, D_MODEL), F32),
        compiler_params=_params("parallel"),
        name="ple_final",
    )(x2, p2, g_ple, w_gate, w_proj, g_final)


def kernel(x, p, g_mix, w_in, conv_w, conv_b, b_i, b_f, lam_q1, lam_k1, lam_q2, lam_k2,
           da_sub_g, ml_norm_g, w_pa, w_pb, w_o, g_mlp, w_up, w_down, g_ple, w_ple_gate,
           w_ple_proj, g_final):
    bsz, seq, _ = x.shape
    t = bsz * seq
    assert w_in.shape[0] == 1, "single-layer block only"
    lambda_init = 0.8 - 0.6 * math.exp(-0.3 * 0)
    x2 = x.reshape(t, D_MODEL)
    row = lambda v: v.reshape(1, -1).astype(F32)

    w = w_in[0]
    c_mq, c_mv, c_mo, c_mi, c_ga = 3072, 5120, 6144, 7168, 7176
    w_main = jnp.concatenate(
        [w[:, :c_mq], w[:, c_mv:c_mo], w[:, c_ga:], w[:, c_mo:c_mi], w[:, c_mq:c_mv]],
        axis=1).astype(BF16)
    w_if = jnp.pad(w[:, c_mi:c_ga], ((0, 0), (0, LANES - 2 * ML_HEADS))).astype(BF16)
    bias_if = jnp.concatenate([b_i[0], b_f[0]]).astype(F32)
    lam4 = jnp.stack([lam_q1[0], lam_k1[0], lam_q2[0], lam_k2[0]]).astype(F32)

    zb, zqk, zif = _in_proj(x2, row(g_mix[0]), w_main, w_if, conv_w[0].astype(F32),
                            row(conv_b[0]), seq=seq)
    ya = _attention(zb, lam4, da_sub_g[0].reshape(DA_V_DIM, 1).astype(F32), bsz=bsz, seq=seq,
                    lambda_init=lambda_init)
    yb = _mlstm(zb, zqk, zif, bias_if, row(ml_norm_g[0]), bsz=bsz, seq=seq)
    x2 = _merge(ya, yb, zb, x2, w_pa[0].astype(BF16), w_pb[0].astype(BF16), w_o[0].astype(BF16))
    x2 = _mlp(x2, row(g_mlp[0]), w_up[0].astype(BF16), w_down[0].astype(BF16))
    x2 = _ple(x2, p[0].reshape(t, PLE_DIM), row(g_ple[0]), w_ple_gate[0].astype(BF16),
              w_ple_proj[0].astype(BF16), row(g_final))
    return x2.reshape(bsz, seq, D_MODEL)
```

```python
import functools
import math

import jax
import jax.numpy as jnp
from jax import lax
from jax.experimental import pallas as pl
from jax.experimental.pallas import tpu as pltpu

F32 = jnp.float32
BF16 = jnp.bfloat16

D_MODEL = 2048
CHUNK = 64
EPS = 1e-6
DA_HEADS = 8
DA_V_DIM = 128
DA_QK_DIM = 64
DA_WIDTH = DA_HEADS * DA_V_DIM
ML_HEADS = 4
ML_HEAD_DIM = 256
ML_WIDTH = ML_HEADS * ML_HEAD_DIM
CONV_K = 4
D_FF = 4 * D_MODEL
PLE_DIM = 256
NEG_INF = -1e30

LANES = 128
SUBLANES = 8
VMEM_LIMIT_BYTES = 56 * 1024 * 1024

ZB_GA, ZB_GB, ZB_AQ, ZB_AK, ZB_AV, ZB_MV, ZB_MO = 0, 2048, 4096, 5120, 6144, 7168, 8192
ZB_COLS = 9216
ZQK_COLS = 2 * ML_WIDTH
ATTN_Q_SCALE = DA_QK_DIM ** -0.5 * math.log2(math.e)


def _rms(x, g):
    ms = jnp.mean(x * x, axis=-1, keepdims=True)
    return (x * lax.rsqrt(ms + EPS)) * g


def _dot(a, b):
    return jnp.dot(a, b, preferred_element_type=F32)


def _dot_nt(a, b):
    return lax.dot_general(a, b, (((1,), (1,)), ((), ())), preferred_element_type=F32)


def _dot_tn(a, b):
    return lax.dot_general(a, b, (((0,), (0,)), ((), ())), preferred_element_type=F32)


def _params(*sem):
    return pltpu.CompilerParams(dimension_semantics=sem, vmem_limit_bytes=VMEM_LIMIT_BYTES)


IN_TN = 1024
IN_NB_GATE = 2 * D_MODEL // IN_TN
IN_J_AQ = IN_NB_GATE
IN_NB_BF16 = ZB_COLS // IN_TN


def _in_proj_kernel(x_ref, g_ref, wa_ref, wg_ref, wif_ref, cw_ref, cb_ref, zb_ref, zqk_ref, zif_ref,
                    h_ref, tail_ref, *, tiles_per_seq):
    i = pl.program_id(0)
    j = pl.program_id(1)
    tm = x_ref.shape[0]

    @pl.when(j == 0)
    def _():
        hb = _rms(x_ref[...], g_ref[...]).astype(BF16)
        h_ref[...] = hb
        zif_ref[...] = _dot(hb, wif_ref[...])

    @pl.when(j < IN_NB_GATE)
    def _():
        zb_ref[...] = _dot(h_ref[...], wg_ref[...]).astype(BF16)

    @pl.when(j == IN_J_AQ)
    def _():
        zb_ref[...] = (_dot(h_ref[...], wa_ref[...]) * ATTN_Q_SCALE).astype(BF16)

    @pl.when((j > IN_J_AQ) & (j < IN_NB_BF16))
    def _():
        zb_ref[...] = _dot(h_ref[...], wa_ref[...]).astype(BF16)

    @pl.when(j >= IN_NB_BF16)
    def _():
        jj = j - IN_NB_BF16
        z = _dot(h_ref[...], wa_ref[...])
        prev = jnp.where(i % tiles_per_seq == 0, 0.0, tail_ref[jj])
        tail_ref[jj] = z[tm - SUBLANES:, :]
        xw = jnp.concatenate([prev, z], axis=0)
        y = cb_ref[...]
        for tap in range(CONV_K):
            off = SUBLANES - (CONV_K - 1) + tap
            y = y + cw_ref[tap:tap + 1, :] * xw[off:off + tm, :]
        zqk_ref[...] = (y * jax.nn.sigmoid(y)).astype(BF16)


def _in_proj(x2, g_mix, w_a, w_g, w_if, conv_w, conv_b, *, seq, tm=1024):
    t = x2.shape[0]
    tn = IN_TN
    nb = IN_NB_BF16 + ZQK_COLS // tn

    def wa_blk(i, j):
        ja = j - IN_J_AQ
        return 0, jnp.where(ja <= 2, jnp.maximum(ja, 0), jnp.where(ja <= 4, ja + 2, ja - 2))

    qk_blk = lambda i, j: (0, jnp.maximum(j - IN_NB_BF16, 0))
    return pl.pallas_call(
        functools.partial(_in_proj_kernel, tiles_per_seq=seq // tm),
        grid=(t // tm, nb),
        in_specs=[
            pl.BlockSpec((tm, D_MODEL), lambda i, j: (i, 0), pipeline_mode=pl.Buffered(1)),
            pl.BlockSpec((1, D_MODEL), lambda i, j: (0, 0)),
            pl.BlockSpec((D_MODEL, tn), wa_blk),
            pl.BlockSpec((D_MODEL, tn), lambda i, j: (0, jnp.minimum(j, IN_NB_GATE - 1))),
            pl.BlockSpec((D_MODEL, LANES), lambda i, j: (0, 0)),
            pl.BlockSpec((CONV_K, tn), qk_blk),
            pl.BlockSpec((1, tn), qk_blk),
        ],
        out_specs=[
            pl.BlockSpec((tm, tn), lambda i, j: (i, jnp.minimum(j, IN_NB_BF16 - 1))),
            pl.BlockSpec((tm, tn), lambda i, j: (i, jnp.maximum(j - IN_NB_BF16, 0))),
            pl.BlockSpec((tm, LANES), lambda i, j: (i, 0)),
        ],
        out_shape=[
            jax.ShapeDtypeStruct((t, ZB_COLS), BF16),
            jax.ShapeDtypeStruct((t, ZQK_COLS), BF16),
            jax.ShapeDtypeStruct((t, LANES), F32),
        ],
        scratch_shapes=[
            pltpu.VMEM((tm, D_MODEL), BF16),
            pltpu.VMEM((ZQK_COLS // tn, SUBLANES, tn), F32),
        ],
        compiler_params=_params("arbitrary", "arbitrary"),
        name="in_proj",
    )(x2, g_mix, w_a, w_g, w_if, conv_w, conv_b)


def _attn_kernel(lam_ref, q_ref, k_ref, v_ref, g_ref, o_ref, vt_ref, s0_ref, s1_ref, acc_ref,
                 *, tq, tk, lambda_init):
    seq = q_ref.shape[0]
    dv = DA_V_DIM

    for kb in range(seq // tk):
        vt_ref[kb, :dv, :] = v_ref[kb * tk:(kb + 1) * tk, :].astype(F32).T.astype(BF16)
        vt_ref[kb, dv:, :] = jnp.ones((vt_ref.shape[1] - dv, tk), BF16)

    lq = lam_ref[...]
    t1 = jnp.sum(lq[0:1, :] * lq[1:2, :], axis=-1, keepdims=True)
    t2 = jnp.sum(lq[2:3, :] * lq[3:4, :], axis=-1, keepdims=True)
    lam = jnp.exp(t1) - jnp.exp(t2) + lambda_init
    s_bufs = (s0_ref, s1_ref)

    for qi in range(seq // tq):
        q = q_ref[qi * tq:(qi + 1) * tq, :]
        lane = lax.broadcasted_iota(jnp.int32, q.shape, 1)
        zero = jnp.zeros_like(q)
        q_maps = (jnp.where(lane < DA_QK_DIM, q, zero), jnp.where(lane >= DA_QK_DIM, q, zero))
        n_kb = (qi + 1) * tq // tk
        n_full = qi * tq // tk

        def scores(kb):
            k = k_ref[kb * tk:(kb + 1) * tk, :]
            for mp in range(2):
                s_bufs[kb % 2][mp] = _dot_nt(k, q_maps[mp])

        m_run = [jnp.full((1, tq), NEG_INF, F32)] * 2
        scores(0)
        for kb in range(n_kb):
            if kb + 1 < n_kb:
                scores(kb + 1)
            for mp in range(2):
                s = s_bufs[kb % 2][mp]
                if kb >= n_full:
                    key = lax.broadcasted_iota(jnp.int32, (tk, tq), 0) + (kb - n_full) * tk
                    qry = lax.broadcasted_iota(jnp.int32, (tk, tq), 1)
                    s = jnp.where((key // CHUNK) <= (qry // CHUNK), s, NEG_INF)
                blk_max = jnp.max(s.reshape(tk // SUBLANES, SUBLANES, tq).max(axis=0),
                                  axis=0, keepdims=True)
                m_next = jnp.maximum(m_run[mp], blk_max)
                p = jnp.exp2(s - m_next).astype(BF16)
                pv = _dot(vt_ref[kb], p)
                if kb == 0:
                    acc_ref[mp] = pv
                else:
                    acc_ref[mp] = jnp.exp2(m_run[mp] - m_next) * acc_ref[mp] + pv
                m_run[mp] = m_next

        a1 = acc_ref[0]
        a2 = acc_ref[1]
        o = a1[:dv] / a1[dv:dv + 1] - lam * (a2[:dv] / a2[dv:dv + 1])
        ms = jnp.mean(o * o, axis=0, keepdims=True)
        y = ((o * lax.rsqrt(ms + EPS)) * g_ref[...]) * (1.0 - lambda_init)
        o_ref[qi * tq:(qi + 1) * tq, :] = y.T.astype(o_ref.dtype)


def _attention(zb, lam4, sub_g_col, *, bsz, seq, lambda_init, tq=512, tk=256):
    kq, kk, kv = ZB_AQ // DA_V_DIM, ZB_AK // DA_V_DIM, ZB_AV // DA_V_DIM
    vt_rows = DA_V_DIM + 16
    return pl.pallas_call(
        functools.partial(_attn_kernel, tq=tq, tk=tk, lambda_init=lambda_init),
        grid=(bsz, DA_HEADS),
        in_specs=[
            pl.BlockSpec((4, DA_QK_DIM), lambda b, h: (0, 0)),
            pl.BlockSpec((seq, DA_V_DIM), lambda b, h: (b, kq + h)),
            pl.BlockSpec((seq, DA_V_DIM), lambda b, h: (b, kk + h)),
            pl.BlockSpec((seq, DA_V_DIM), lambda b, h: (b, kv + h)),
            pl.BlockSpec((DA_V_DIM, 1), lambda b, h: (0, 0)),
        ],
        out_specs=pl.BlockSpec((seq, DA_V_DIM), lambda b, h: (b, h)),
        out_shape=jax.ShapeDtypeStruct((bsz * seq, DA_WIDTH), BF16),
        scratch_shapes=[
            pltpu.VMEM((seq // tk, vt_rows, tk), BF16),
            pltpu.VMEM((2, tk, tq), F32),
            pltpu.VMEM((2, tk, tq), F32),
            pltpu.VMEM((2, vt_rows, tq), F32),
        ],
        compiler_params=_params("parallel", "parallel"),
        name="diff_attn",
    )(lam4, zb, zb, zb, sub_g_col)


def _log_sigmoid(x):
    return jnp.minimum(x, 0.0) - jnp.log1p(jnp.exp(-jnp.abs(x)))


def _mlstm_kernel(bias_ref, q_ref, k_ref, v_ref, og_ref, zif_ref, g_ref, y_ref, c_ref, *, seq):
    L = CHUNK
    d = ML_HEAD_DIM
    h = pl.program_id(1)
    b_i = bias_ref[h]
    b_f = bias_ref[ML_HEADS + h]
    c_ref[...] = jnp.zeros(c_ref.shape, F32)

    lane = lax.broadcasted_iota(jnp.int32, (L, LANES), 1)
    sel_i = lane == h
    sel_f = lane == h + ML_HEADS
    row = lax.broadcasted_iota(jnp.int32, (L, L), 0)
    col = lax.broadcasted_iota(jnp.int32, (L, L), 1)
    causal = col <= row
    eye = col == row
    qscale = d ** -0.5

    def chunk(c, carry):
        m_prev, n_prev = carry
        t0 = pl.multiple_of(c * L, L)
        z = zif_ref[pl.ds(t0, L), :]
        i_col = jnp.sum(jnp.where(sel_i, z, 0.0), axis=1, keepdims=True) + b_i
        f_col = _log_sigmoid(jnp.sum(jnp.where(sel_f, z, 0.0), axis=1, keepdims=True) + b_f)
        f_mat = jnp.broadcast_to(f_col, (L, L))
        i_mat = jnp.broadcast_to(i_col, (L, L))
        b_row = jnp.sum(jnp.where(row <= col, f_mat, 0.0), axis=0, keepdims=True)
        i_row = jnp.sum(jnp.where(eye, i_mat, 0.0), axis=0, keepdims=True)
        b_col = jnp.sum(jnp.where(eye, jnp.broadcast_to(b_row, (L, L)), 0.0), axis=1, keepdims=True)

        dlog = jnp.where(causal, b_col - b_row + i_row, NEG_INF)
        inter_log = b_col + m_prev
        m_t = jnp.maximum(inter_log, jnp.max(dlog, axis=1, keepdims=True))
        dmat = jnp.exp(dlog - m_t)
        inter_w = jnp.exp(inter_log - m_t)

        qb = q_ref[pl.ds(t0, L), :]
        kb = k_ref[pl.ds(t0, L), :]
        q = qb.astype(F32)
        k = kb.astype(F32)
        v = v_ref[pl.ds(t0, L), :]

        s = (_dot_nt(qb, kb) * qscale) * dmat
        inter = _dot(qb, c_ref[...].astype(BF16)) * qscale
        num = _dot(s.astype(BF16), v) + inter_w * inter
        qn = jnp.sum(q * n_prev, axis=1, keepdims=True) * qscale
        den = jnp.sum(s, axis=1, keepdims=True) + inter_w * qn
        hh = num / jnp.maximum(jnp.abs(den), jnp.exp(-m_t))

        hn = _rms(hh, g_ref[...])
        og = og_ref[pl.ds(t0, L), :].astype(F32)
        y_ref[pl.ds(t0, L), :] = (jax.nn.sigmoid(og) * hn).astype(y_ref.dtype)

        b_last = b_col[L - 1:L, :]
        upd = b_last - b_col + i_col
        m_new = jnp.maximum(b_last + m_prev, jnp.max(upd, axis=0, keepdims=True))
        w_col = jnp.exp(upd - m_new)
        decay = jnp.exp(b_last + m_prev - m_new)
        kw = k * w_col
        c_ref[...] = decay * c_ref[...] + _dot_tn(kw.astype(BF16), v)
        n_new = decay * n_prev + jnp.sum(kw, axis=0, keepdims=True)
        return m_new, n_new

    lax.fori_loop(0, seq // L, chunk, (jnp.zeros((1, 1), F32), jnp.zeros((1, d), F32)), unroll=8)


def _mlstm(zb, zqk, zif, bias_if, norm_g, *, bsz, seq):
    d = ML_HEAD_DIM
    kmv, kmo = ZB_MV // d, ZB_MO // d
    return pl.pallas_call(
        functools.partial(_mlstm_kernel, seq=seq),
        grid=(bsz, ML_HEADS),
        in_specs=[
            pl.BlockSpec(memory_space=pltpu.SMEM),
            pl.BlockSpec((seq, d), lambda b, h: (b, h)),
            pl.BlockSpec((seq, d), lambda b, h: (b, ML_HEADS + h)),
            pl.BlockSpec((seq, d), lambda b, h: (b, kmv + h)),
            pl.BlockSpec((seq, d), lambda b, h: (b, kmo + h)),
            pl.BlockSpec((seq, LANES), lambda b, h: (b, 0)),
            pl.BlockSpec((1, d), lambda b, h: (0, h)),
        ],
        out_specs=pl.BlockSpec((seq, d), lambda b, h: (b, h)),
        out_shape=jax.ShapeDtypeStruct((bsz * seq, ML_WIDTH), BF16),
        scratch_shapes=[pltpu.VMEM((d, d), F32)],
        compiler_params=_params("parallel", "parallel"),
        name="mlstm",
    )(bias_if, zqk, zqk, zb, zb, zif, norm_g)


def _merge_kernel(ya_ref, yb_ref, ga_ref, gb_ref, x_ref, wpa_ref, wpb_ref, wo_ref, o_ref):
    pa = _dot(ya_ref[...], wpa_ref[...])
    pb = _dot(yb_ref[...], wpb_ref[...])
    merged = (jax.nn.sigmoid(ga_ref[...].astype(F32)) * pa
              + jax.nn.sigmoid(gb_ref[...].astype(F32)) * pb)
    o_ref[...] = x_ref[...] + _dot(merged.astype(BF16), wo_ref[...])


def _resident(shape):
    return pl.BlockSpec(shape, lambda *_: (0,) * len(shape), pipeline_mode=pl.Buffered(1))


def _merge(ya, yb, zb, x2, w_pa, w_pb, w_o, *, tm=256):
    t = x2.shape[0]
    kga, kgb = ZB_GA // D_MODEL, ZB_GB // D_MODEL
    return pl.pallas_call(
        _merge_kernel,
        grid=(t // tm,),
        in_specs=[
            pl.BlockSpec((tm, DA_WIDTH), lambda i: (i, 0)),
            pl.BlockSpec((tm, ML_WIDTH), lambda i: (i, 0)),
            pl.BlockSpec((tm, D_MODEL), lambda i: (i, kga)),
            pl.BlockSpec((tm, D_MODEL), lambda i: (i, kgb)),
            pl.BlockSpec((tm, D_MODEL), lambda i: (i, 0)),
            _resident((DA_WIDTH, D_MODEL)),
            _resident((ML_WIDTH, D_MODEL)),
            _resident((D_MODEL, D_MODEL)),
        ],
        out_specs=pl.BlockSpec((tm, D_MODEL), lambda i: (i, 0)),
        out_shape=jax.ShapeDtypeStruct((t, D_MODEL), F32),
        compiler_params=_params("parallel"),
        name="merge",
    )(ya, yb, zb, zb, x2, w_pa, w_pb, w_o)


def _mlp_kernel(x_ref, g_ref, wup_ref, wdn_ref, o_ref, h_ref):
    f = pl.program_id(1)

    @pl.when(f == 0)
    def _():
        x = x_ref[...]
        h_ref[...] = _rms(x, g_ref[...]).astype(BF16)
        o_ref[...] = x

    u = _dot(h_ref[...], wup_ref[...])
    a = jnp.square(jnp.maximum(u, 0.0)).astype(BF16)
    o_ref[...] += _dot(a, wdn_ref[...])


def _mlp(x1, g_mlp, w_up, w_down, *, tm=1024, tf=512):
    t = x1.shape[0]
    return pl.pallas_call(
        _mlp_kernel,
        grid=(t // tm, D_FF // tf),
        in_specs=[
            pl.BlockSpec((tm, D_MODEL), lambda i, f: (i, 0)),
            pl.BlockSpec((1, D_MODEL), lambda i, f: (0, 0)),
            pl.BlockSpec((D_MODEL, tf), lambda i, f: (0, f)),
            pl.BlockSpec((tf, D_MODEL), lambda i, f: (f, 0)),
        ],
        out_specs=pl.BlockSpec((tm, D_MODEL), lambda i, f: (i, 0)),
        out_shape=jax.ShapeDtypeStruct((t, D_MODEL), F32),
        scratch_shapes=[pltpu.VMEM((tm, D_MODEL), BF16)],
        compiler_params=_params("parallel", "arbitrary"),
        name="mlp",
    )(x1, g_mlp, w_up, w_down)


def _ple_kernel(x_ref, p_ref, gp_ref, wg_ref, wp_ref, gf_ref, o_ref):
    x = x_ref[...]
    gate = jax.nn.sigmoid(_dot(_rms(x, gp_ref[...]).astype(BF16), wg_ref[...]))
    proj = _dot(p_ref[...].astype(BF16), wp_ref[...])
    o_ref[...] = _rms(x + gate * proj, gf_ref[...])


def _ple(x2, p2, g_ple, w_gate, w_proj, g_final, *, tm=512):
    t = x2.shape[0]
    return pl.pallas_call(
        _ple_kernel,
        grid=(t // tm,),
        in_specs=[
            pl.BlockSpec((tm, D_MODEL), lambda i: (i, 0)),
            pl.BlockSpec((tm, PLE_DIM), lambda i: (i, 0)),
            pl.BlockSpec((1, D_MODEL), lambda i: (0, 0)),
            _resident((D_MODEL, D_MODEL)),
            _resident((PLE_DIM, D_MODEL)),
            pl.BlockSpec((1, D_MODEL), lambda i: (0, 0)),
        ],
        out_specs=pl.BlockSpec((tm, D_MODEL), lambda i: (i, 0)),
        out_shape=jax.ShapeDtypeStruct((t, D_MODEL), F32),
        compiler_params=_params("parallel"),
        name="ple_final",
    )(x2, p2, g_ple, w_gate, w_proj, g_final)


def kernel(x, p, g_mix, w_in, conv_w, conv_b, b_i, b_f, lam_q1, lam_k1, lam_q2, lam_k2,
           da_sub_g, ml_norm_g, w_pa, w_pb, w_o, g_mlp, w_up, w_down, g_ple, w_ple_gate,
           w_ple_proj, g_final):
    bsz, seq, _ = x.shape
    t = bsz * seq
    assert w_in.shape[0] == 1, "single-layer block only"
    lambda_init = 0.8 - 0.6 * math.exp(-0.3 * 0)
    x2 = x.reshape(t, D_MODEL)
    row = lambda v: v.reshape(1, -1).astype(F32)

    w = w_in[0]
    c_if, c_gate = 7 * 1024, 7 * 1024 + 2 * ML_HEADS
    w_a = w[:, :c_if].astype(BF16)
    w_g = w[:, c_gate:].astype(BF16)
    w_if = jnp.pad(w[:, c_if:c_gate], ((0, 0), (0, LANES - 2 * ML_HEADS))).astype(BF16)
    bias_if = jnp.concatenate([b_i[0], b_f[0]]).astype(F32)
    lam4 = jnp.stack([lam_q1[0], lam_k1[0], lam_q2[0], lam_k2[0]]).astype(F32)

    zb, zqk, zif = _in_proj(x2, row(g_mix[0]), w_a, w_g, w_if, conv_w[0].astype(F32),
                            row(conv_b[0]), seq=seq)
    ya = _attention(zb, lam4, da_sub_g[0].reshape(DA_V_DIM, 1).astype(F32), bsz=bsz, seq=seq,
                    lambda_init=lambda_init)
    yb = _mlstm(zb, zqk, zif, bias_if, row(ml_norm_g[0]), bsz=bsz, seq=seq)
    x2 = _merge(ya, yb, zb, x2, w_pa[0].astype(BF16), w_pb[0].astype(BF16), w_o[0].astype(BF16))
    x2 = _mlp(x2, row(g_mlp[0]), w_up[0].astype(BF16), w_down[0].astype(BF16))
    x2 = _ple(x2, p[0].reshape(t, PLE_DIM), row(g_ple[0]), w_ple_gate[0].astype(BF16),
              w_ple_proj[0].astype(BF16), row(g_final))
    return x2.reshape(bsz, seq, D_MODEL)
```

```python
import functools
import math

import jax
import jax.numpy as jnp
from jax import lax
from jax.experimental import pallas as pl
from jax.experimental.pallas import tpu as pltpu

F32 = jnp.float32
BF16 = jnp.bfloat16

D_MODEL = 2048
CHUNK = 64
EPS = 1e-6
DA_HEADS = 8
DA_V_DIM = 128
DA_QK_DIM = 64
DA_WIDTH = DA_HEADS * DA_V_DIM
ML_HEADS = 4
ML_HEAD_DIM = 256
ML_WIDTH = ML_HEADS * ML_HEAD_DIM
CONV_K = 4
D_FF = 4 * D_MODEL
PLE_DIM = 256
NEG_INF = -1e30

LANES = 128
SUBLANES = 8
VMEM_LIMIT_BYTES = 56 * 1024 * 1024

ZB_GA, ZB_GB, ZB_AQ, ZB_AK, ZB_AV, ZB_MV, ZB_MO = 0, 2048, 4096, 5120, 6144, 7168, 8192
ZB_COLS = 9216
ZQK_COLS = 2 * ML_WIDTH
ATTN_Q_SCALE = DA_QK_DIM ** -0.5 * math.log2(math.e)


def _rms(x, g):
    ms = jnp.mean(x * x, axis=-1, keepdims=True)
    return (x * lax.rsqrt(ms + EPS)) * g


def _dot(a, b):
    return jnp.dot(a, b, preferred_element_type=F32)


def _dot_nt(a, b):
    return lax.dot_general(a, b, (((1,), (1,)), ((), ())), preferred_element_type=F32)


def _dot_tn(a, b):
    return lax.dot_general(a, b, (((0,), (0,)), ((), ())), preferred_element_type=F32)


def _params(*sem):
    return pltpu.CompilerParams(dimension_semantics=sem, vmem_limit_bytes=VMEM_LIMIT_BYTES)


IN_TN = 1024
IN_NB_GATE = 2 * D_MODEL // IN_TN
IN_J_AQ = IN_NB_GATE
IN_NB_BF16 = ZB_COLS // IN_TN


def _in_proj_kernel(x_ref, g_ref, wa_ref, wg_ref, wif_ref, cw_ref, cb_ref, zb_ref, zqk_ref, zif_ref,
                    h_ref, tail_ref, *, tiles_per_seq):
    i = pl.program_id(0)
    j = pl.program_id(1)
    tm = x_ref.shape[0]

    @pl.when(j == 0)
    def _():
        hb = _rms(x_ref[...], g_ref[...]).astype(BF16)
        h_ref[...] = hb
        zif_ref[...] = _dot(hb, wif_ref[...])

    @pl.when(j < IN_NB_GATE)
    def _():
        zb_ref[...] = _dot(h_ref[...], wg_ref[...]).astype(BF16)

    @pl.when(j == IN_J_AQ)
    def _():
        zb_ref[...] = (_dot(h_ref[...], wa_ref[...]) * ATTN_Q_SCALE).astype(BF16)

    @pl.when((j > IN_J_AQ) & (j < IN_NB_BF16))
    def _():
        zb_ref[...] = _dot(h_ref[...], wa_ref[...]).astype(BF16)

    @pl.when(j >= IN_NB_BF16)
    def _():
        jj = j - IN_NB_BF16
        z = _dot(h_ref[...], wa_ref[...])
        prev = jnp.where(i % tiles_per_seq == 0, 0.0, tail_ref[jj])
        tail_ref[jj] = z[tm - SUBLANES:, :]
        xw = jnp.concatenate([prev, z], axis=0)
        y = cb_ref[...]
        for tap in range(CONV_K):
            off = SUBLANES - (CONV_K - 1) + tap
            y = y + cw_ref[tap:tap + 1, :] * xw[off:off + tm, :]
        zqk_ref[...] = (y * jax.nn.sigmoid(y)).astype(BF16)


def _in_proj(x2, g_mix, w_a, w_g, w_if, conv_w, conv_b, *, seq, tm=1024):
    t = x2.shape[0]
    tn = IN_TN
    nb = IN_NB_BF16 + ZQK_COLS // tn

    def wa_blk(i, j):
        ja = j - IN_J_AQ
        return 0, jnp.where(ja <= 2, jnp.maximum(ja, 0), jnp.where(ja <= 4, ja + 2, ja - 2))

    qk_blk = lambda i, j: (0, jnp.maximum(j - IN_NB_BF16, 0))
    return pl.pallas_call(
        functools.partial(_in_proj_kernel, tiles_per_seq=seq // tm),
        grid=(t // tm, nb),
        in_specs=[
            pl.BlockSpec((tm, D_MODEL), lambda i, j: (i, 0)),
            pl.BlockSpec((1, D_MODEL), lambda i, j: (0, 0)),
            pl.BlockSpec((D_MODEL, tn), wa_blk),
            pl.BlockSpec((D_MODEL, tn), lambda i, j: (0, jnp.minimum(j, IN_NB_GATE - 1))),
            pl.BlockSpec((D_MODEL, LANES), lambda i, j: (0, 0)),
            pl.BlockSpec((CONV_K, tn), qk_blk),
            pl.BlockSpec((1, tn), qk_blk),
        ],
        out_specs=[
            pl.BlockSpec((tm, tn), lambda i, j: (i, jnp.minimum(j, IN_NB_BF16 - 1))),
            pl.BlockSpec((tm, tn), lambda i, j: (i, jnp.maximum(j - IN_NB_BF16, 0))),
            pl.BlockSpec((tm, LANES), lambda i, j: (i, 0)),
        ],
        out_shape=[
            jax.ShapeDtypeStruct((t, ZB_COLS), BF16),
            jax.ShapeDtypeStruct((t, ZQK_COLS), BF16),
            jax.ShapeDtypeStruct((t, LANES), F32),
        ],
        scratch_shapes=[
            pltpu.VMEM((tm, D_MODEL), BF16),
            pltpu.VMEM((ZQK_COLS // tn, SUBLANES, tn), F32),
        ],
        compiler_params=_params("arbitrary", "arbitrary"),
        name="in_proj",
    )(x2, g_mix, w_a, w_g, w_if, conv_w, conv_b)


def _attn_kernel(lam_ref, q_ref, k_ref, v_ref, g_ref, o_ref, vt_ref, s0_ref, s1_ref, acc_ref,
                 *, tq, tk, lambda_init):
    seq = q_ref.shape[0]
    dv = DA_V_DIM

    for kb in range(seq // tk):
        vt_ref[kb, :dv, :] = v_ref[kb * tk:(kb + 1) * tk, :].astype(F32).T.astype(BF16)
        vt_ref[kb, dv:, :] = jnp.ones((vt_ref.shape[1] - dv, tk), BF16)

    lq = lam_ref[...]
    t1 = jnp.sum(lq[0:1, :] * lq[1:2, :], axis=-1, keepdims=True)
    t2 = jnp.sum(lq[2:3, :] * lq[3:4, :], axis=-1, keepdims=True)
    lam = jnp.exp(t1) - jnp.exp(t2) + lambda_init
    s_bufs = (s0_ref, s1_ref)

    for qi in range(seq // tq):
        q = q_ref[qi * tq:(qi + 1) * tq, :]
        lane = lax.broadcasted_iota(jnp.int32, q.shape, 1)
        zero = jnp.zeros_like(q)
        q_maps = (jnp.where(lane < DA_QK_DIM, q, zero), jnp.where(lane >= DA_QK_DIM, q, zero))
        n_kb = (qi + 1) * tq // tk
        n_full = qi * tq // tk

        def first_query(kb):
            return max(kb - n_full, 0) * tk

        def scores(kb):
            k = k_ref[kb * tk:(kb + 1) * tk, :]
            lo = first_query(kb)
            for mp in range(2):
                s_bufs[kb % 2][mp, :, :tq - lo] = _dot_nt(k, q_maps[mp][lo:, :])

        m_run = [jnp.full((1, tq), NEG_INF, F32)] * 2
        scores(0)
        for kb in range(n_kb):
            if kb + 1 < n_kb:
                scores(kb + 1)
            lo = first_query(kb)
            w = tq - lo
            for mp in range(2):
                s = s_bufs[kb % 2][mp, :, :w]
                if kb >= n_full:
                    key = lax.broadcasted_iota(jnp.int32, (tk, w), 0)
                    qry = lax.broadcasted_iota(jnp.int32, (tk, w), 1)
                    s = jnp.where((key // CHUNK) <= (qry // CHUNK), s, NEG_INF)
                blk_max = jnp.max(s.reshape(tk // SUBLANES, SUBLANES, w).max(axis=0),
                                  axis=0, keepdims=True)
                m_prev = m_run[mp][:, lo:]
                m_next = jnp.maximum(m_prev, blk_max)
                p = jnp.exp2(s - m_next).astype(BF16)
                pv = _dot(vt_ref[kb], p)
                if kb == 0:
                    acc_ref[mp] = pv
                else:
                    acc_ref[mp, :, lo:] = jnp.exp2(m_prev - m_next) * acc_ref[mp, :, lo:] + pv
                m_run[mp] = jnp.concatenate([m_run[mp][:, :lo], m_next], axis=1) if lo else m_next

        a1 = acc_ref[0]
        a2 = acc_ref[1]
        o = a1[:dv] / a1[dv:dv + 1] - lam * (a2[:dv] / a2[dv:dv + 1])
        ms = jnp.mean(o * o, axis=0, keepdims=True)
        y = ((o * lax.rsqrt(ms + EPS)) * g_ref[...]) * (1.0 - lambda_init)
        o_ref[qi * tq:(qi + 1) * tq, :] = y.T.astype(o_ref.dtype)


def _attention(zb, lam4, sub_g_col, *, bsz, seq, lambda_init, tq=512, tk=256):
    kq, kk, kv = ZB_AQ // DA_V_DIM, ZB_AK // DA_V_DIM, ZB_AV // DA_V_DIM
    vt_rows = DA_V_DIM + 16
    return pl.pallas_call(
        functools.partial(_attn_kernel, tq=tq, tk=tk, lambda_init=lambda_init),
        grid=(bsz, DA_HEADS),
        in_specs=[
            pl.BlockSpec((4, DA_QK_DIM), lambda b, h: (0, 0)),
            pl.BlockSpec((seq, DA_V_DIM), lambda b, h: (b, kq + h)),
            pl.BlockSpec((seq, DA_V_DIM), lambda b, h: (b, kk + h)),
            pl.BlockSpec((seq, DA_V_DIM), lambda b, h: (b, kv + h)),
            pl.BlockSpec((DA_V_DIM, 1), lambda b, h: (0, 0)),
        ],
        out_specs=pl.BlockSpec((seq, DA_V_DIM), lambda b, h: (b, h)),
        out_shape=jax.ShapeDtypeStruct((bsz * seq, DA_WIDTH), BF16),
        scratch_shapes=[
            pltpu.VMEM((seq // tk, vt_rows, tk), BF16),
            pltpu.VMEM((2, tk, tq), F32),
            pltpu.VMEM((2, tk, tq), F32),
            pltpu.VMEM((2, vt_rows, tq), F32),
        ],
        compiler_params=_params("parallel", "parallel"),
        name="diff_attn",
    )(lam4, zb, zb, zb, sub_g_col)


def _log_sigmoid(x):
    return jnp.minimum(x, 0.0) - jnp.log(1.0 + jnp.exp(-jnp.abs(x)))


def _mlstm_kernel(bias_ref, q_ref, k_ref, v_ref, og_ref, zif_ref, g_ref, y_ref, c_ref, *, seq):
    L = CHUNK
    d = ML_HEAD_DIM
    h = pl.program_id(1)
    b_i = bias_ref[h]
    b_f = bias_ref[ML_HEADS + h]
    c_ref[...] = jnp.zeros(c_ref.shape, F32)

    lane = lax.broadcasted_iota(jnp.int32, (L, LANES), 1)
    sel_i = lane == h
    sel_f = lane == h + ML_HEADS
    row = lax.broadcasted_iota(jnp.int32, (L, L), 0)
    col = lax.broadcasted_iota(jnp.int32, (L, L), 1)
    causal = col <= row
    eye = col == row
    qscale = d ** -0.5

    def chunk(c, carry):
        m_prev, n_prev = carry
        t0 = pl.multiple_of(c * L, L)
        z = zif_ref[pl.ds(t0, L), :]
        i_col = jnp.sum(jnp.where(sel_i, z, 0.0), axis=1, keepdims=True) + b_i
        f_col = _log_sigmoid(jnp.sum(jnp.where(sel_f, z, 0.0), axis=1, keepdims=True) + b_f)
        f_mat = jnp.broadcast_to(f_col, (L, L))
        i_mat = jnp.broadcast_to(i_col, (L, L))
        b_row = jnp.sum(jnp.where(row <= col, f_mat, 0.0), axis=0, keepdims=True)
        i_row = jnp.sum(jnp.where(eye, i_mat, 0.0), axis=0, keepdims=True)
        b_col = jnp.sum(jnp.where(eye, jnp.broadcast_to(b_row, (L, L)), 0.0), axis=1, keepdims=True)

        dlog = jnp.where(causal, b_col - b_row + i_row, NEG_INF)
        inter_log = b_col + m_prev
        m_t = jnp.maximum(inter_log, jnp.max(dlog, axis=1, keepdims=True))
        dmat = jnp.exp(dlog - m_t)
        inter_w = jnp.exp(inter_log - m_t)

        qb = q_ref[pl.ds(t0, L), :]
        kb = k_ref[pl.ds(t0, L), :]
        q = qb.astype(F32)
        k = kb.astype(F32)
        v = v_ref[pl.ds(t0, L), :]

        s = (_dot_nt(qb, kb) * qscale) * dmat
        inter = _dot(qb, c_ref[...].astype(BF16)) * qscale
        num = _dot(s.astype(BF16), v) + inter_w * inter
        qn = jnp.sum(q * n_prev, axis=1, keepdims=True) * qscale
        den = jnp.sum(s, axis=1, keepdims=True) + inter_w * qn
        hh = num / jnp.maximum(jnp.abs(den), jnp.exp(-m_t))

        hn = _rms(hh, g_ref[...])
        og = og_ref[pl.ds(t0, L), :].astype(F32)
        y_ref[pl.ds(t0, L), :] = (jax.nn.sigmoid(og) * hn).astype(y_ref.dtype)

        b_last = b_col[L - 1:L, :]
        upd = b_last - b_col + i_col
        m_new = jnp.maximum(b_last + m_prev, jnp.max(upd, axis=0, keepdims=True))
        w_col = jnp.exp(upd - m_new)
        decay = jnp.exp(b_last + m_prev - m_new)
        kw = k * w_col
        c_ref[...] = decay * c_ref[...] + _dot_tn(kw.astype(BF16), v)
        n_new = decay * n_prev + jnp.sum(kw, axis=0, keepdims=True)
        return m_new, n_new

    lax.fori_loop(0, seq // L, chunk, (jnp.zeros((1, 1), F32), jnp.zeros((1, d), F32)), unroll=8)


def _mlstm(zb, zqk, zif, bias_if, norm_g, *, bsz, seq):
    d = ML_HEAD_DIM
    kmv, kmo = ZB_MV // d, ZB_MO // d
    return pl.pallas_call(
        functools.partial(_mlstm_kernel, seq=seq),
        grid=(bsz, ML_HEADS),
        in_specs=[
            pl.BlockSpec(memory_space=pltpu.SMEM),
            pl.BlockSpec((seq, d), lambda b, h: (b, h)),
            pl.BlockSpec((seq, d), lambda b, h: (b, ML_HEADS + h)),
            pl.BlockSpec((seq, d), lambda b, h: (b, kmv + h)),
            pl.BlockSpec((seq, d), lambda b, h: (b, kmo + h)),
            pl.BlockSpec((seq, LANES), lambda b, h: (b, 0)),
            pl.BlockSpec((1, d), lambda b, h: (0, h)),
        ],
        out_specs=pl.BlockSpec((seq, d), lambda b, h: (b, h)),
        out_shape=jax.ShapeDtypeStruct((bsz * seq, ML_WIDTH), BF16),
        scratch_shapes=[pltpu.VMEM((d, d), F32)],
        compiler_params=_params("parallel", "parallel"),
        name="mlstm",
    )(bias_if, zqk, zqk, zb, zb, zif, norm_g)


def _merge_kernel(ya_ref, yb_ref, ga_ref, gb_ref, x_ref, wpa_ref, wpb_ref, wo_ref, o_ref):
    pa = _dot(ya_ref[...], wpa_ref[...])
    pb = _dot(yb_ref[...], wpb_ref[...])
    merged = (jax.nn.sigmoid(ga_ref[...].astype(F32)) * pa
              + jax.nn.sigmoid(gb_ref[...].astype(F32)) * pb)
    o_ref[...] = x_ref[...] + _dot(merged.astype(BF16), wo_ref[...])


def _resident(shape):
    return pl.BlockSpec(shape, lambda *_: (0,) * len(shape), pipeline_mode=pl.Buffered(1))


def _merge(ya, yb, zb, x2, w_pa, w_pb, w_o, *, tm=256):
    t = x2.shape[0]
    kga, kgb = ZB_GA // D_MODEL, ZB_GB // D_MODEL
    return pl.pallas_call(
        _merge_kernel,
        grid=(t // tm,),
        in_specs=[
            pl.BlockSpec((tm, DA_WIDTH), lambda i: (i, 0)),
            pl.BlockSpec((tm, ML_WIDTH), lambda i: (i, 0)),
            pl.BlockSpec((tm, D_MODEL), lambda i: (i, kga)),
            pl.BlockSpec((tm, D_MODEL), lambda i: (i, kgb)),
            pl.BlockSpec((tm, D_MODEL), lambda i: (i, 0)),
            _resident((DA_WIDTH, D_MODEL)),
            _resident((ML_WIDTH, D_MODEL)),
            _resident((D_MODEL, D_MODEL)),
        ],
        out_specs=pl.BlockSpec((tm, D_MODEL), lambda i: (i, 0)),
        out_shape=jax.ShapeDtypeStruct((t, D_MODEL), F32),
        compiler_params=_params("parallel"),
        name="merge",
    )(ya, yb, zb, zb, x2, w_pa, w_pb, w_o)


def _mlp_kernel(x_ref, g_ref, wup_ref, wdn_ref, o_ref, h_ref):
    f = pl.program_id(1)

    @pl.when(f == 0)
    def _():
        x = x_ref[...]
        h_ref[...] = _rms(x, g_ref[...]).astype(BF16)
        o_ref[...] = x

    u = _dot(h_ref[...], wup_ref[...])
    a = jnp.square(jnp.maximum(u, 0.0)).astype(BF16)
    o_ref[...] += _dot(a, wdn_ref[...])


def _mlp(x1, g_mlp, w_up, w_down, *, tm=1024, tf=512):
    t = x1.shape[0]
    return pl.pallas_call(
        _mlp_kernel,
        grid=(t // tm, D_FF // tf),
        in_specs=[
            pl.BlockSpec((tm, D_MODEL), lambda i, f: (i, 0)),
            pl.BlockSpec((1, D_MODEL), lambda i, f: (0, 0)),
            pl.BlockSpec((D_MODEL, tf), lambda i, f: (0, f)),
            pl.BlockSpec((tf, D_MODEL), lambda i, f: (f, 0)),
        ],
        out_specs=pl.BlockSpec((tm, D_MODEL), lambda i, f: (i, 0)),
        out_shape=jax.ShapeDtypeStruct((t, D_MODEL), F32),
        scratch_shapes=[pltpu.VMEM((tm, D_MODEL), BF16)],
        compiler_params=_params("parallel", "arbitrary"),
        name="mlp",
    )(x1, g_mlp, w_up, w_down)


def _ple_kernel(x_ref, p_ref, gp_ref, wg_ref, wp_ref, gf_ref, o_ref):
    x = x_ref[...]
    gate = jax.nn.sigmoid(_dot(_rms(x, gp_ref[...]).astype(BF16), wg_ref[...]))
    proj = _dot(p_ref[...].astype(BF16), wp_ref[...])
    o_ref[...] = _rms(x + gate * proj, gf_ref[...])


def _ple(x2, p2, g_ple, w_gate, w_proj, g_final, *, tm=512):
    t = x2.shape[0]
    return pl.pallas_call(
        _ple_kernel,
        grid=(t // tm,),
        in_specs=[
            pl.BlockSpec((tm, D_MODEL), lambda i: (i, 0)),
            pl.BlockSpec((tm, PLE_DIM), lambda i: (i, 0)),
            pl.BlockSpec((1, D_MODEL), lambda i: (0, 0)),
            _resident((D_MODEL, D_MODEL)),
            _resident((PLE_DIM, D_MODEL)),
            pl.BlockSpec((1, D_MODEL), lambda i: (0, 0)),
        ],
        out_specs=pl.BlockSpec((tm, D_MODEL), lambda i: (i, 0)),
        out_shape=jax.ShapeDtypeStruct((t, D_MODEL), F32),
        compiler_params=_params("parallel"),
        name="ple_final",
    )(x2, p2, g_ple, w_gate, w_proj, g_final)


def kernel(x, p, g_mix, w_in, conv_w, conv_b, b_i, b_f, lam_q1, lam_k1, lam_q2, lam_k2,
           da_sub_g, ml_norm_g, w_pa, w_pb, w_o, g_mlp, w_up, w_down, g_ple, w_ple_gate,
           w_ple_proj, g_final):
    bsz, seq, _ = x.shape
    t = bsz * seq
    assert w_in.shape[0] == 1, "single-layer block only"
    lambda_init = 0.8 - 0.6 * math.exp(-0.3 * 0)
    x2 = x.reshape(t, D_MODEL)
    row = lambda v: v.reshape(1, -1).astype(F32)

    w_a = w_in[0].astype(BF16)
    c_if, c_gate = 7 * 1024, 7 * 1024 + 2 * ML_HEADS
    w_g = w_a[:, c_gate:]
    w_if = jnp.pad(w_a[:, c_if:c_gate], ((0, 0), (0, LANES - 2 * ML_HEADS)))
    bias_if = jnp.concatenate([b_i[0], b_f[0]]).astype(F32)
    lam4 = jnp.stack([lam_q1[0], lam_k1[0], lam_q2[0], lam_k2[0]]).astype(F32)

    zb, zqk, zif = _in_proj(x2, row(g_mix[0]), w_a, w_g, w_if, conv_w[0].astype(F32),
                            row(conv_b[0]), seq=seq)
    ya = _attention(zb, lam4, da_sub_g[0].reshape(DA_V_DIM, 1).astype(F32), bsz=bsz, seq=seq,
                    lambda_init=lambda_init)
    yb = _mlstm(zb, zqk, zif, bias_if, row(ml_norm_g[0]), bsz=bsz, seq=seq)
    x2 = _merge(ya, yb, zb, x2, w_pa[0].astype(BF16), w_pb[0].astype(BF16), w_o[0].astype(BF16))
    x2 = _mlp(x2, row(g_mlp[0]), w_up[0].astype(BF16), w_down[0].astype(BF16))
    x2 = _ple(x2, p[0].reshape(t, PLE_DIM), row(g_ple[0]), w_ple_gate[0].astype(BF16),
              w_ple_proj[0].astype(BF16), row(g_final))
    return x2.reshape(bsz, seq, D_MODEL)
```

```python
import functools
import math

import jax
import jax.numpy as jnp
from jax import lax
from jax.experimental import pallas as pl
from jax.experimental.pallas import tpu as pltpu

F32 = jnp.float32
BF16 = jnp.bfloat16

D_MODEL = 2048
CHUNK = 64
EPS = 1e-6
DA_HEADS = 8
DA_V_DIM = 128
DA_QK_DIM = 64
DA_WIDTH = DA_HEADS * DA_V_DIM
ML_HEADS = 4
ML_HEAD_DIM = 256
ML_WIDTH = ML_HEADS * ML_HEAD_DIM
CONV_K = 4
D_FF = 4 * D_MODEL
PLE_DIM = 256
NEG_INF = -1e30

LANES = 128
SUBLANES = 8
VMEM_LIMIT_BYTES = 56 * 1024 * 1024

ZB_GA, ZB_GB, ZB_AQ, ZB_AK, ZB_AV, ZB_MV, ZB_MO = 0, 2048, 4096, 5120, 6144, 7168, 8192
ZB_COLS = 9216
ZQK_COLS = 2 * ML_WIDTH
ATTN_Q_SCALE = DA_QK_DIM ** -0.5 * math.log2(math.e)


def _rms(x, g):
    ms = jnp.mean(x * x, axis=-1, keepdims=True)
    return (x * lax.rsqrt(ms + EPS)) * g


def _dot(a, b):
    return jnp.dot(a, b, preferred_element_type=F32)


def _dot_nt(a, b):
    return lax.dot_general(a, b, (((1,), (1,)), ((), ())), preferred_element_type=F32)


def _dot_tn(a, b):
    return lax.dot_general(a, b, (((0,), (0,)), ((), ())), preferred_element_type=F32)


def _params(*sem):
    return pltpu.CompilerParams(dimension_semantics=sem, vmem_limit_bytes=VMEM_LIMIT_BYTES)


IN_TN = 1024
IN_NB_GATE = 2 * D_MODEL // IN_TN
IN_J_AQ = IN_NB_GATE
IN_NB_BF16 = ZB_COLS // IN_TN


def _in_proj_kernel(x_ref, g_ref, wa_ref, wg_ref, wif_ref, cw_ref, cb_ref, zb_ref, zqk_ref, zif_ref,
                    h_ref, tail_ref, *, tiles_per_seq):
    i = pl.program_id(0)
    j = pl.program_id(1)
    tm = x_ref.shape[0]

    @pl.when(j == 0)
    def _():
        hb = _rms(x_ref[...], g_ref[...]).astype(BF16)
        h_ref[...] = hb
        zif_ref[...] = _dot(hb, wif_ref[...])

    @pl.when(j < IN_NB_GATE)
    def _():
        zb_ref[...] = _dot(h_ref[...], wg_ref[...]).astype(BF16)

    @pl.when(j == IN_J_AQ)
    def _():
        zb_ref[...] = (_dot(h_ref[...], wa_ref[...]) * ATTN_Q_SCALE).astype(BF16)

    @pl.when((j > IN_J_AQ) & (j < IN_NB_BF16))
    def _():
        zb_ref[...] = _dot(h_ref[...], wa_ref[...]).astype(BF16)

    @pl.when(j >= IN_NB_BF16)
    def _():
        jj = j - IN_NB_BF16
        z = _dot(h_ref[...], wa_ref[...])
        prev = jnp.where(i % tiles_per_seq == 0, 0.0, tail_ref[jj])
        tail_ref[jj] = z[tm - SUBLANES:, :]
        xw = jnp.concatenate([prev, z], axis=0)
        y = cb_ref[...]
        for tap in range(CONV_K):
            off = SUBLANES - (CONV_K - 1) + tap
            y = y + cw_ref[tap:tap + 1, :] * xw[off:off + tm, :]
        zqk_ref[...] = (y * jax.nn.sigmoid(y)).astype(BF16)


def _in_proj(x2, g_mix, w_a, w_g, w_if, conv_w, conv_b, *, seq, tm=1024):
    t = x2.shape[0]
    tn = IN_TN
    nb = IN_NB_BF16 + ZQK_COLS // tn

    def wa_blk(i, j):
        ja = j - IN_J_AQ
        return 0, jnp.where(ja <= 2, jnp.maximum(ja, 0), jnp.where(ja <= 4, ja + 2, ja - 2))

    qk_blk = lambda i, j: (0, jnp.maximum(j - IN_NB_BF16, 0))
    return pl.pallas_call(
        functools.partial(_in_proj_kernel, tiles_per_seq=seq // tm),
        grid=(t // tm, nb),
        in_specs=[
            pl.BlockSpec((tm, D_MODEL), lambda i, j: (i, 0)),
            pl.BlockSpec((1, D_MODEL), lambda i, j: (0, 0)),
            pl.BlockSpec((D_MODEL, tn), wa_blk),
            pl.BlockSpec((D_MODEL, tn), lambda i, j: (0, jnp.minimum(j, IN_NB_GATE - 1))),
            pl.BlockSpec((D_MODEL, LANES), lambda i, j: (0, 0)),
            pl.BlockSpec((CONV_K, tn), qk_blk),
            pl.BlockSpec((1, tn), qk_blk),
        ],
        out_specs=[
            pl.BlockSpec((tm, tn), lambda i, j: (i, jnp.minimum(j, IN_NB_BF16 - 1))),
            pl.BlockSpec((tm, tn), lambda i, j: (i, jnp.maximum(j - IN_NB_BF16, 0))),
            pl.BlockSpec((tm, LANES), lambda i, j: (i, 0)),
        ],
        out_shape=[
            jax.ShapeDtypeStruct((t, ZB_COLS), BF16),
            jax.ShapeDtypeStruct((t, ZQK_COLS), BF16),
            jax.ShapeDtypeStruct((t, LANES), F32),
        ],
        scratch_shapes=[
            pltpu.VMEM((tm, D_MODEL), BF16),
            pltpu.VMEM((ZQK_COLS // tn, SUBLANES, tn), F32),
        ],
        compiler_params=_params("arbitrary", "arbitrary"),
        name="in_proj",
    )(x2, g_mix, w_a, w_g, w_if, conv_w, conv_b)


def _attn_kernel(lam_ref, q_ref, k_ref, v_ref, g_ref, *rest, tq, tk, lambda_init, n_cast):
    cast_in, (o_ref, *cast_out) = rest[:n_cast], rest[n_cast:2 * n_cast + 1]
    vt_ref, s0_ref, s1_ref, acc_ref = rest[2 * n_cast + 1:]
    for src, dst in zip(cast_in, cast_out):
        dst[...] = src[...].astype(BF16)
    seq = q_ref.shape[0]
    dv = DA_V_DIM

    for kb in range(seq // tk):
        vt_ref[kb, :dv, :] = v_ref[kb * tk:(kb + 1) * tk, :].astype(F32).T.astype(BF16)
        vt_ref[kb, dv:, :] = jnp.ones((vt_ref.shape[1] - dv, tk), BF16)

    lq = lam_ref[...]
    t1 = jnp.sum(lq[0:1, :] * lq[1:2, :], axis=-1, keepdims=True)
    t2 = jnp.sum(lq[2:3, :] * lq[3:4, :], axis=-1, keepdims=True)
    lam = jnp.exp(t1) - jnp.exp(t2) + lambda_init
    s_bufs = (s0_ref, s1_ref)

    for qi in range(seq // tq):
        q = q_ref[qi * tq:(qi + 1) * tq, :]
        lane = lax.broadcasted_iota(jnp.int32, q.shape, 1)
        zero = jnp.zeros_like(q)
        q_maps = (jnp.where(lane < DA_QK_DIM, q, zero), jnp.where(lane >= DA_QK_DIM, q, zero))
        n_kb = (qi + 1) * tq // tk
        n_full = qi * tq // tk

        def first_query(kb):
            return max(kb - n_full, 0) * tk

        def scores(kb):
            k = k_ref[kb * tk:(kb + 1) * tk, :]
            lo = first_query(kb)
            for mp in range(2):
                s_bufs[kb % 2][mp, :, :tq - lo] = _dot_nt(k, q_maps[mp][lo:, :])

        m_run = [jnp.full((1, tq), NEG_INF, F32)] * 2
        scores(0)
        for kb in range(n_kb):
            if kb + 1 < n_kb:
                scores(kb + 1)
            lo = first_query(kb)
            w = tq - lo
            for mp in range(2):
                s = s_bufs[kb % 2][mp, :, :w]
                if kb >= n_full:
                    key = lax.broadcasted_iota(jnp.int32, (tk, w), 0)
                    qry = lax.broadcasted_iota(jnp.int32, (tk, w), 1)
                    s = jnp.where((key // CHUNK) <= (qry // CHUNK), s, NEG_INF)
                blk_max = jnp.max(s.reshape(tk // SUBLANES, SUBLANES, w).max(axis=0),
                                  axis=0, keepdims=True)
                m_prev = m_run[mp][:, lo:]
                m_next = jnp.maximum(m_prev, blk_max)
                p = jnp.exp2(s - m_next).astype(BF16)
                pv = _dot(vt_ref[kb], p)
                if kb == 0:
                    acc_ref[mp] = pv
                else:
                    acc_ref[mp, :, lo:] = jnp.exp2(m_prev - m_next) * acc_ref[mp, :, lo:] + pv
                m_run[mp] = jnp.concatenate([m_run[mp][:, :lo], m_next], axis=1) if lo else m_next

        a1 = acc_ref[0]
        a2 = acc_ref[1]
        o = a1[:dv] / a1[dv:dv + 1] - lam * (a2[:dv] / a2[dv:dv + 1])
        ms = jnp.mean(o * o, axis=0, keepdims=True)
        y = ((o * lax.rsqrt(ms + EPS)) * g_ref[...]) * (1.0 - lambda_init)
        o_ref[qi * tq:(qi + 1) * tq, :] = y.T.astype(o_ref.dtype)


def _attention(zb, lam4, sub_g_col, cast_ws, *, bsz, seq, lambda_init, tq=512, tk=256):
    kq, kk, kv = ZB_AQ // DA_V_DIM, ZB_AK // DA_V_DIM, ZB_AV // DA_V_DIM
    vt_rows = DA_V_DIM + 16
    steps = bsz * DA_HEADS
    slab = lambda b, h: (b * DA_HEADS + h, 0)
    cast_specs = [pl.BlockSpec((w.shape[0] // steps, w.shape[1]), slab) for w in cast_ws]
    assert all(w.shape[0] % (16 * steps) == 0 for w in cast_ws)
    return pl.pallas_call(
        functools.partial(_attn_kernel, tq=tq, tk=tk, lambda_init=lambda_init, n_cast=len(cast_ws)),
        grid=(bsz, DA_HEADS),
        in_specs=[
            pl.BlockSpec((4, DA_QK_DIM), lambda b, h: (0, 0)),
            pl.BlockSpec((seq, DA_V_DIM), lambda b, h: (b, kq + h)),
            pl.BlockSpec((seq, DA_V_DIM), lambda b, h: (b, kk + h)),
            pl.BlockSpec((seq, DA_V_DIM), lambda b, h: (b, kv + h)),
            pl.BlockSpec((DA_V_DIM, 1), lambda b, h: (0, 0)),
            *cast_specs,
        ],
        out_specs=[pl.BlockSpec((seq, DA_V_DIM), lambda b, h: (b, h)), *cast_specs],
        out_shape=[jax.ShapeDtypeStruct((bsz * seq, DA_WIDTH), BF16),
                   *[jax.ShapeDtypeStruct(w.shape, BF16) for w in cast_ws]],
        scratch_shapes=[
            pltpu.VMEM((seq // tk, vt_rows, tk), BF16),
            pltpu.VMEM((2, tk, tq), F32),
            pltpu.VMEM((2, tk, tq), F32),
            pltpu.VMEM((2, vt_rows, tq), F32),
        ],
        compiler_params=_params("parallel", "parallel"),
        name="diff_attn",
    )(lam4, zb, zb, zb, sub_g_col, *cast_ws)


def _log_sigmoid(x):
    return jnp.minimum(x, 0.0) - jnp.log(1.0 + jnp.exp(-jnp.abs(x)))


def _mlstm_kernel(bias_ref, q_ref, k_ref, v_ref, og_ref, zif_ref, g_ref, y_ref, c_ref, *, seq):
    L = CHUNK
    d = ML_HEAD_DIM
    h = pl.program_id(1)
    b_i = bias_ref[h]
    b_f = bias_ref[ML_HEADS + h]
    c_ref[...] = jnp.zeros(c_ref.shape, F32)

    lane = lax.broadcasted_iota(jnp.int32, (L, LANES), 1)
    sel_i = lane == h
    sel_f = lane == h + ML_HEADS
    row = lax.broadcasted_iota(jnp.int32, (L, L), 0)
    col = lax.broadcasted_iota(jnp.int32, (L, L), 1)
    causal = col <= row
    eye = col == row
    qscale = d ** -0.5

    def chunk(c, carry):
        m_prev, n_prev = carry
        t0 = pl.multiple_of(c * L, L)
        z = zif_ref[pl.ds(t0, L), :]
        i_col = jnp.sum(jnp.where(sel_i, z, 0.0), axis=1, keepdims=True) + b_i
        f_col = _log_sigmoid(jnp.sum(jnp.where(sel_f, z, 0.0), axis=1, keepdims=True) + b_f)
        f_mat = jnp.broadcast_to(f_col, (L, L))
        i_mat = jnp.broadcast_to(i_col, (L, L))
        b_row = jnp.sum(jnp.where(row <= col, f_mat, 0.0), axis=0, keepdims=True)
        i_row = jnp.sum(jnp.where(eye, i_mat, 0.0), axis=0, keepdims=True)
        b_col = jnp.sum(jnp.where(eye, jnp.broadcast_to(b_row, (L, L)), 0.0), axis=1, keepdims=True)

        dlog = jnp.where(causal, b_col - b_row + i_row, NEG_INF)
        inter_log = b_col + m_prev
        m_t = jnp.maximum(inter_log, jnp.max(dlog, axis=1, keepdims=True))
        dmat = jnp.exp(dlog - m_t)
        inter_w = jnp.exp(inter_log - m_t)

        qb = q_ref[pl.ds(t0, L), :]
        kb = k_ref[pl.ds(t0, L), :]
        q = qb.astype(F32)
        k = kb.astype(F32)
        v = v_ref[pl.ds(t0, L), :]

        s = (_dot_nt(qb, kb) * qscale) * dmat
        inter = _dot(qb, c_ref[...].astype(BF16)) * qscale
        num = _dot(s.astype(BF16), v) + inter_w * inter
        qn = jnp.sum(q * n_prev, axis=1, keepdims=True) * qscale
        den = jnp.sum(s, axis=1, keepdims=True) + inter_w * qn
        hh = num / jnp.maximum(jnp.abs(den), jnp.exp(-m_t))

        hn = _rms(hh, g_ref[...])
        og = og_ref[pl.ds(t0, L), :].astype(F32)
        y_ref[pl.ds(t0, L), :] = (jax.nn.sigmoid(og) * hn).astype(y_ref.dtype)

        b_last = b_col[L - 1:L, :]
        upd = b_last - b_col + i_col
        m_new = jnp.maximum(b_last + m_prev, jnp.max(upd, axis=0, keepdims=True))
        w_col = jnp.exp(upd - m_new)
        decay = jnp.exp(b_last + m_prev - m_new)
        kw = k * w_col
        c_ref[...] = decay * c_ref[...] + _dot_tn(kw.astype(BF16), v)
        n_new = decay * n_prev + jnp.sum(kw, axis=0, keepdims=True)
        return m_new, n_new

    lax.fori_loop(0, seq // L, chunk, (jnp.zeros((1, 1), F32), jnp.zeros((1, d), F32)), unroll=8)


def _mlstm(zb, zqk, zif, bias_if, norm_g, *, bsz, seq):
    d = ML_HEAD_DIM
    kmv, kmo = ZB_MV // d, ZB_MO // d
    return pl.pallas_call(
        functools.partial(_mlstm_kernel, seq=seq),
        grid=(bsz, ML_HEADS),
        in_specs=[
            pl.BlockSpec(memory_space=pltpu.SMEM),
            pl.BlockSpec((seq, d), lambda b, h: (b, h)),
            pl.BlockSpec((seq, d), lambda b, h: (b, ML_HEADS + h)),
            pl.BlockSpec((seq, d), lambda b, h: (b, kmv + h)),
            pl.BlockSpec((seq, d), lambda b, h: (b, kmo + h)),
            pl.BlockSpec((seq, LANES), lambda b, h: (b, 0)),
            pl.BlockSpec((1, d), lambda b, h: (0, h)),
        ],
        out_specs=pl.BlockSpec((seq, d), lambda b, h: (b, h)),
        out_shape=jax.ShapeDtypeStruct((bsz * seq, ML_WIDTH), BF16),
        scratch_shapes=[pltpu.VMEM((d, d), F32)],
        compiler_params=_params("parallel", "parallel"),
        name="mlstm",
    )(bias_if, zqk, zqk, zb, zb, zif, norm_g)


def _merge_kernel(ya_ref, yb_ref, ga_ref, gb_ref, x_ref, wpa_ref, wpb_ref, wo_ref, wup_ref, wdn_ref,
                  o_ref, wup_bf_ref, wdn_bf_ref):
    pa = _dot(ya_ref[...], wpa_ref[...])
    pb = _dot(yb_ref[...], wpb_ref[...])
    merged = (jax.nn.sigmoid(ga_ref[...].astype(F32)) * pa
              + jax.nn.sigmoid(gb_ref[...].astype(F32)) * pb)
    o_ref[...] = x_ref[...] + _dot(merged.astype(BF16), wo_ref[...])
    wup_bf_ref[...] = wup_ref[...].astype(BF16)
    wdn_bf_ref[...] = wdn_ref[...].astype(BF16)


def _resident(shape):
    return pl.BlockSpec(shape, lambda *_: (0,) * len(shape), pipeline_mode=pl.Buffered(1))


def _merge(ya, yb, zb, x2, w_pa, w_pb, w_o, w_up, w_down, *, tm=256):
    t = x2.shape[0]
    steps = t // tm
    fw = D_FF // steps
    assert fw % LANES == 0
    kga, kgb = ZB_GA // D_MODEL, ZB_GB // D_MODEL
    return pl.pallas_call(
        _merge_kernel,
        grid=(steps,),
        in_specs=[
            pl.BlockSpec((tm, DA_WIDTH), lambda i: (i, 0)),
            pl.BlockSpec((tm, ML_WIDTH), lambda i: (i, 0)),
            pl.BlockSpec((tm, D_MODEL), lambda i: (i, kga)),
            pl.BlockSpec((tm, D_MODEL), lambda i: (i, kgb)),
            pl.BlockSpec((tm, D_MODEL), lambda i: (i, 0)),
            _resident((DA_WIDTH, D_MODEL)),
            _resident((ML_WIDTH, D_MODEL)),
            _resident((D_MODEL, D_MODEL)),
            pl.BlockSpec((D_MODEL, fw), lambda i: (0, i)),
            pl.BlockSpec((fw, D_MODEL), lambda i: (i, 0)),
        ],
        out_specs=[
            pl.BlockSpec((tm, D_MODEL), lambda i: (i, 0)),
            pl.BlockSpec((D_MODEL, fw), lambda i: (0, i)),
            pl.BlockSpec((fw, D_MODEL), lambda i: (i, 0)),
        ],
        out_shape=[
            jax.ShapeDtypeStruct((t, D_MODEL), F32),
            jax.ShapeDtypeStruct((D_MODEL, D_FF), BF16),
            jax.ShapeDtypeStruct((D_FF, D_MODEL), BF16),
        ],
        compiler_params=_params("parallel"),
        name="merge",
    )(ya, yb, zb, zb, x2, w_pa, w_pb, w_o, w_up, w_down)


def _mlp_kernel(x_ref, g_ref, wup_ref, wdn_ref, o_ref, h_ref):
    f = pl.program_id(1)

    @pl.when(f == 0)
    def _():
        x = x_ref[...]
        h_ref[...] = _rms(x, g_ref[...]).astype(BF16)
        o_ref[...] = x

    u = _dot(h_ref[...], wup_ref[...])
    a = jnp.square(jnp.maximum(u, 0.0)).astype(BF16)
    o_ref[...] += _dot(a, wdn_ref[...])


def _mlp(x1, g_mlp, w_up, w_down, *, tm=1024, tf=512):
    t = x1.shape[0]
    return pl.pallas_call(
        _mlp_kernel,
        grid=(t // tm, D_FF // tf),
        in_specs=[
            pl.BlockSpec((tm, D_MODEL), lambda i, f: (i, 0)),
            pl.BlockSpec((1, D_MODEL), lambda i, f: (0, 0)),
            pl.BlockSpec((D_MODEL, tf), lambda i, f: (0, f)),
            pl.BlockSpec((tf, D_MODEL), lambda i, f: (f, 0)),
        ],
        out_specs=pl.BlockSpec((tm, D_MODEL), lambda i, f: (i, 0)),
        out_shape=jax.ShapeDtypeStruct((t, D_MODEL), F32),
        scratch_shapes=[pltpu.VMEM((tm, D_MODEL), BF16)],
        compiler_params=_params("parallel", "arbitrary"),
        name="mlp",
    )(x1, g_mlp, w_up, w_down)


def _ple_kernel(x_ref, p_ref, gp_ref, wg_ref, wp_ref, gf_ref, o_ref):
    x = x_ref[...]
    gate = jax.nn.sigmoid(_dot(_rms(x, gp_ref[...]).astype(BF16), wg_ref[...]))
    proj = _dot(p_ref[...].astype(BF16), wp_ref[...])
    o_ref[...] = _rms(x + gate * proj, gf_ref[...])


def _ple(x2, p2, g_ple, w_gate, w_proj, g_final, *, tm=512):
    t = x2.shape[0]
    return pl.pallas_call(
        _ple_kernel,
        grid=(t // tm,),
        in_specs=[
            pl.BlockSpec((tm, D_MODEL), lambda i: (i, 0)),
            pl.BlockSpec((tm, PLE_DIM), lambda i: (i, 0)),
            pl.BlockSpec((1, D_MODEL), lambda i: (0, 0)),
            _resident((D_MODEL, D_MODEL)),
            _resident((PLE_DIM, D_MODEL)),
            pl.BlockSpec((1, D_MODEL), lambda i: (0, 0)),
        ],
        out_specs=pl.BlockSpec((tm, D_MODEL), lambda i: (i, 0)),
        out_shape=jax.ShapeDtypeStruct((t, D_MODEL), F32),
        compiler_params=_params("parallel"),
        name="ple_final",
    )(x2, p2, g_ple, w_gate, w_proj, g_final)


def kernel(x, p, g_mix, w_in, conv_w, conv_b, b_i, b_f, lam_q1, lam_k1, lam_q2, lam_k2,
           da_sub_g, ml_norm_g, w_pa, w_pb, w_o, g_mlp, w_up, w_down, g_ple, w_ple_gate,
           w_ple_proj, g_final):
    bsz, seq, _ = x.shape
    t = bsz * seq
    assert w_in.shape[0] == 1, "single-layer block only"
    lambda_init = 0.8 - 0.6 * math.exp(-0.3 * 0)
    x2 = x.reshape(t, D_MODEL)
    row = lambda v: v.reshape(1, -1).astype(F32)

    w_a = w_in[0].astype(BF16)
    c_if, c_gate = 7 * 1024, 7 * 1024 + 2 * ML_HEADS
    w_g = w_a[:, c_gate:]
    w_if = jnp.pad(w_a[:, c_if:c_gate], ((0, 0), (0, LANES - 2 * ML_HEADS)))
    bias_if = jnp.concatenate([b_i[0], b_f[0]]).astype(F32)
    lam4 = jnp.stack([lam_q1[0], lam_k1[0], lam_q2[0], lam_k2[0]]).astype(F32)

    zb, zqk, zif = _in_proj(x2, row(g_mix[0]), w_a, w_g, w_if, conv_w[0].astype(F32),
                            row(conv_b[0]), seq=seq)
    ya, w_pa_bf, w_pb_bf, w_o_bf, w_pg_bf = _attention(
        zb, lam4, da_sub_g[0].reshape(DA_V_DIM, 1).astype(F32),
        [w_pa[0], w_pb[0], w_o[0], w_ple_gate[0]], bsz=bsz, seq=seq, lambda_init=lambda_init)
    yb = _mlstm(zb, zqk, zif, bias_if, row(ml_norm_g[0]), bsz=bsz, seq=seq)
    x2, w_up_bf, w_down_bf = _merge(ya, yb, zb, x2, w_pa_bf, w_pb_bf, w_o_bf, w_up[0], w_down[0])
    x2 = _mlp(x2, row(g_mlp[0]), w_up_bf, w_down_bf)
    x2 = _ple(x2, p[0].reshape(t, PLE_DIM), row(g_ple[0]), w_pg_bf,
              w_ple_proj[0].astype(BF16), row(g_final))
    return x2.reshape(bsz, seq, D_MODEL)
```

```python
import functools
import math

import jax
import jax.numpy as jnp
from jax import lax
from jax.experimental import pallas as pl
from jax.experimental.pallas import tpu as pltpu

F32 = jnp.float32
BF16 = jnp.bfloat16

D_MODEL = 2048
CHUNK = 64
EPS = 1e-6
DA_HEADS = 8
DA_V_DIM = 128
DA_QK_DIM = 64
DA_WIDTH = DA_HEADS * DA_V_DIM
ML_HEADS = 4
ML_HEAD_DIM = 256
ML_WIDTH = ML_HEADS * ML_HEAD_DIM
CONV_K = 4
D_FF = 4 * D_MODEL
PLE_DIM = 256
NEG_INF = -1e30

LANES = 128
SUBLANES = 8
VMEM_LIMIT_BYTES = 56 * 1024 * 1024
MLP_VMEM_LIMIT_BYTES = 60 * 1024 * 1024

ZB_GA, ZB_GB, ZB_AQ, ZB_AK, ZB_AV, ZB_MV, ZB_MO = 0, 2048, 4096, 5120, 6144, 7168, 8192
ZB_COLS = 9216
ZQK_COLS = 2 * ML_WIDTH
ATTN_Q_SCALE = DA_QK_DIM ** -0.5 * math.log2(math.e)


def _rms(x, g):
    ms = jnp.mean(x * x, axis=-1, keepdims=True)
    return (x * lax.rsqrt(ms + EPS)) * g


def _dot(a, b):
    return jnp.dot(a, b, preferred_element_type=F32)


def _dot_nt(a, b):
    return lax.dot_general(a, b, (((1,), (1,)), ((), ())), preferred_element_type=F32)


def _dot_tn(a, b):
    return lax.dot_general(a, b, (((0,), (0,)), ((), ())), preferred_element_type=F32)


def _params(*sem, vmem_limit_bytes=VMEM_LIMIT_BYTES):
    return pltpu.CompilerParams(dimension_semantics=sem, vmem_limit_bytes=vmem_limit_bytes)


IN_TN = 1024
IN_NB_GATE = 2 * D_MODEL // IN_TN
IN_J_AQ = IN_NB_GATE
IN_NB_BF16 = ZB_COLS // IN_TN


def _in_proj_kernel(x_ref, g_ref, wa_ref, wg_ref, wif_ref, cw_ref, cb_ref, zb_ref, zqk_ref, zif_ref,
                    h_ref, tail_ref, *, tiles_per_seq):
    i = pl.program_id(0)
    j = pl.program_id(1)
    tm = x_ref.shape[0]

    @pl.when(j == 0)
    def _():
        hb = _rms(x_ref[...], g_ref[...]).astype(BF16)
        h_ref[...] = hb
        zif_ref[...] = _dot_nt(hb, wif_ref[...])

    @pl.when(j < IN_NB_GATE)
    def _():
        zb_ref[...] = _dot_nt(h_ref[...], wg_ref[...]).astype(BF16)

    @pl.when(j == IN_J_AQ)
    def _():
        zb_ref[...] = (_dot_nt(h_ref[...], wa_ref[...]) * ATTN_Q_SCALE).astype(BF16)

    @pl.when((j > IN_J_AQ) & (j < IN_NB_BF16))
    def _():
        zb_ref[...] = _dot_nt(h_ref[...], wa_ref[...]).astype(BF16)

    @pl.when(j >= IN_NB_BF16)
    def _():
        jj = j - IN_NB_BF16
        z = _dot_nt(h_ref[...], wa_ref[...])
        prev = jnp.where(i % tiles_per_seq == 0, 0.0, tail_ref[jj])
        tail_ref[jj] = z[tm - SUBLANES:, :]
        xw = jnp.concatenate([prev, z], axis=0)
        y = cb_ref[...]
        for tap in range(CONV_K):
            off = SUBLANES - (CONV_K - 1) + tap
            y = y + cw_ref[tap:tap + 1, :] * xw[off:off + tm, :]
        zqk_ref[...] = (y * jax.nn.sigmoid(y)).astype(BF16)


def _in_proj(x2, g_mix, w_a, w_g, w_if, conv_w, conv_b, *, seq, tm=1024):
    t = x2.shape[0]
    tn = IN_TN
    nb = IN_NB_BF16 + ZQK_COLS // tn

    def wa_blk(i, j):
        ja = j - IN_J_AQ
        return jnp.where(ja <= 2, jnp.maximum(ja, 0), jnp.where(ja <= 4, ja + 2, ja - 2)), 0

    qk_blk = lambda i, j: (0, jnp.maximum(j - IN_NB_BF16, 0))
    return pl.pallas_call(
        functools.partial(_in_proj_kernel, tiles_per_seq=seq // tm),
        grid=(t // tm, nb),
        in_specs=[
            pl.BlockSpec((tm, D_MODEL), lambda i, j: (i, 0)),
            pl.BlockSpec((1, D_MODEL), lambda i, j: (0, 0)),
            pl.BlockSpec((tn, D_MODEL), wa_blk),
            pl.BlockSpec((tn, D_MODEL), lambda i, j: (jnp.minimum(j, IN_NB_GATE - 1), 0)),
            pl.BlockSpec((LANES, D_MODEL), lambda i, j: (0, 0)),
            pl.BlockSpec((CONV_K, tn), qk_blk),
            pl.BlockSpec((1, tn), qk_blk),
        ],
        out_specs=[
            pl.BlockSpec((tm, tn), lambda i, j: (i, jnp.minimum(j, IN_NB_BF16 - 1))),
            pl.BlockSpec((tm, tn), lambda i, j: (i, jnp.maximum(j - IN_NB_BF16, 0))),
            pl.BlockSpec((tm, LANES), lambda i, j: (i, 0)),
        ],
        out_shape=[
            jax.ShapeDtypeStruct((t, ZB_COLS), BF16),
            jax.ShapeDtypeStruct((t, ZQK_COLS), BF16),
            jax.ShapeDtypeStruct((t, LANES), F32),
        ],
        scratch_shapes=[
            pltpu.VMEM((tm, D_MODEL), BF16),
            pltpu.VMEM((ZQK_COLS // tn, SUBLANES, tn), F32),
        ],
        compiler_params=_params("arbitrary", "arbitrary"),
        name="in_proj",
    )(x2, g_mix, w_a, w_g, w_if, conv_w, conv_b)


def _attn_kernel(lam_ref, q_ref, k_ref, v_ref, g_ref, *rest, tq, tk, lambda_init, n_cast):
    cast_in, (o_ref, *cast_out) = rest[:n_cast], rest[n_cast:2 * n_cast + 1]
    vt_ref, s0_ref, s1_ref, acc_ref = rest[2 * n_cast + 1:]
    for src, dst in zip(cast_in, cast_out):
        dst[...] = src[...].astype(BF16)
    seq = q_ref.shape[0]
    dv = DA_V_DIM

    for kb in range(seq // tk):
        vt_ref[kb, :dv, :] = v_ref[kb * tk:(kb + 1) * tk, :].astype(F32).T.astype(BF16)
        vt_ref[kb, dv:, :] = jnp.ones((vt_ref.shape[1] - dv, tk), BF16)

    lq = lam_ref[...]
    t1 = jnp.sum(lq[0:1, :] * lq[1:2, :], axis=-1, keepdims=True)
    t2 = jnp.sum(lq[2:3, :] * lq[3:4, :], axis=-1, keepdims=True)
    lam = jnp.exp(t1) - jnp.exp(t2) + lambda_init
    s_bufs = (s0_ref, s1_ref)

    for qi in range(seq // tq):
        q = q_ref[qi * tq:(qi + 1) * tq, :]
        lane = lax.broadcasted_iota(jnp.int32, q.shape, 1)
        zero = jnp.zeros_like(q)
        q_maps = (jnp.where(lane < DA_QK_DIM, q, zero), jnp.where(lane >= DA_QK_DIM, q, zero))
        n_kb = (qi + 1) * tq // tk
        n_full = qi * tq // tk

        def first_query(kb):
            return max(kb - n_full, 0) * tk

        def scores(kb):
            k = k_ref[kb * tk:(kb + 1) * tk, :]
            lo = first_query(kb)
            for mp in range(2):
                s_bufs[kb % 2][mp, :, :tq - lo] = _dot_nt(k, q_maps[mp][lo:, :])

        m_run = [jnp.full((1, tq), NEG_INF, F32)] * 2
        scores(0)
        for kb in range(n_kb):
            if kb + 1 < n_kb:
                scores(kb + 1)
            lo = first_query(kb)
            w = tq - lo
            for mp in range(2):
                s = s_bufs[kb % 2][mp, :, :w]
                if kb >= n_full:
                    key = lax.broadcasted_iota(jnp.int32, (tk, w), 0)
                    qry = lax.broadcasted_iota(jnp.int32, (tk, w), 1)
                    s = jnp.where((key // CHUNK) <= (qry // CHUNK), s, NEG_INF)
                blk_max = jnp.max(s.reshape(tk // SUBLANES, SUBLANES, w).max(axis=0),
                                  axis=0, keepdims=True)
                m_prev = m_run[mp][:, lo:]
                m_next = jnp.maximum(m_prev, blk_max)
                p = jnp.exp2(s - m_next).astype(BF16)
                pv = _dot(vt_ref[kb], p)
                if kb == 0:
                    acc_ref[mp] = pv
                else:
                    acc_ref[mp, :, lo:] = jnp.exp2(m_prev - m_next) * acc_ref[mp, :, lo:] + pv
                m_run[mp] = jnp.concatenate([m_run[mp][:, :lo], m_next], axis=1) if lo else m_next

        a1 = acc_ref[0]
        a2 = acc_ref[1]
        o = a1[:dv] / a1[dv:dv + 1] - lam * (a2[:dv] / a2[dv:dv + 1])
        ms = jnp.mean(o * o, axis=0, keepdims=True)
        y = ((o * lax.rsqrt(ms + EPS)) * g_ref[...]) * (1.0 - lambda_init)
        o_ref[qi * tq:(qi + 1) * tq, :] = y.T.astype(o_ref.dtype)


def _attention(zb, lam4, sub_g_col, cast_ws, *, bsz, seq, lambda_init, tq=1024, tk=256):
    kq, kk, kv = ZB_AQ // DA_V_DIM, ZB_AK // DA_V_DIM, ZB_AV // DA_V_DIM
    vt_rows = DA_V_DIM + 16
    steps = bsz * DA_HEADS
    slab = lambda b, h: (b * DA_HEADS + h, 0)
    cast_specs = [pl.BlockSpec((w.shape[0] // steps, w.shape[1]), slab) for w in cast_ws]
    assert all(w.shape[0] % (16 * steps) == 0 for w in cast_ws)
    return pl.pallas_call(
        functools.partial(_attn_kernel, tq=tq, tk=tk, lambda_init=lambda_init, n_cast=len(cast_ws)),
        grid=(bsz, DA_HEADS),
        in_specs=[
            pl.BlockSpec((4, DA_QK_DIM), lambda b, h: (0, 0)),
            pl.BlockSpec((seq, DA_V_DIM), lambda b, h: (b, kq + h)),
            pl.BlockSpec((seq, DA_V_DIM), lambda b, h: (b, kk + h)),
            pl.BlockSpec((seq, DA_V_DIM), lambda b, h: (b, kv + h)),
            pl.BlockSpec((DA_V_DIM, 1), lambda b, h: (0, 0)),
            *cast_specs,
        ],
        out_specs=[pl.BlockSpec((seq, DA_V_DIM), lambda b, h: (b, h)), *cast_specs],
        out_shape=[jax.ShapeDtypeStruct((bsz * seq, DA_WIDTH), BF16),
                   *[jax.ShapeDtypeStruct(w.shape, BF16) for w in cast_ws]],
        scratch_shapes=[
            pltpu.VMEM((seq // tk, vt_rows, tk), BF16),
            pltpu.VMEM((2, tk, tq), F32),
            pltpu.VMEM((2, tk, tq), F32),
            pltpu.VMEM((2, vt_rows, tq), F32),
        ],
        compiler_params=_params("parallel", "parallel"),
        name="diff_attn",
    )(lam4, zb, zb, zb, sub_g_col, *cast_ws)


def _log_sigmoid(x):
    return jnp.minimum(x, 0.0) - jnp.log(1.0 + jnp.exp(-jnp.abs(x)))


def _mlstm_kernel(bias_ref, q_ref, k_ref, v_ref, og_ref, zif_ref, g_ref, y_ref, c_ref, *, seq):
    L = CHUNK
    d = ML_HEAD_DIM
    h = pl.program_id(1)
    b_i = bias_ref[h]
    b_f = bias_ref[ML_HEADS + h]
    c_ref[...] = jnp.zeros(c_ref.shape, F32)

    lane = lax.broadcasted_iota(jnp.int32, (L, LANES), 1)
    sel_i = lane == h
    sel_f = lane == h + ML_HEADS
    row = lax.broadcasted_iota(jnp.int32, (L, L), 0)
    col = lax.broadcasted_iota(jnp.int32, (L, L), 1)
    causal = col <= row
    eye = col == row
    qscale = d ** -0.5

    def chunk(c, carry):
        m_prev, n_prev = carry
        t0 = pl.multiple_of(c * L, L)
        z = zif_ref[pl.ds(t0, L), :]
        i_col = jnp.sum(jnp.where(sel_i, z, 0.0), axis=1, keepdims=True) + b_i
        f_col = _log_sigmoid(jnp.sum(jnp.where(sel_f, z, 0.0), axis=1, keepdims=True) + b_f)
        f_mat = jnp.broadcast_to(f_col, (L, L))
        i_mat = jnp.broadcast_to(i_col, (L, L))
        b_row = jnp.sum(jnp.where(row <= col, f_mat, 0.0), axis=0, keepdims=True)
        i_row = jnp.sum(jnp.where(eye, i_mat, 0.0), axis=0, keepdims=True)
        b_col = jnp.sum(jnp.where(eye, jnp.broadcast_to(b_row, (L, L)), 0.0), axis=1, keepdims=True)

        dlog = jnp.where(causal, b_col - b_row + i_row, NEG_INF)
        inter_log = b_col + m_prev
        m_t = jnp.maximum(inter_log, jnp.max(dlog, axis=1, keepdims=True))
        dmat = jnp.exp(dlog - m_t)
        inter_w = jnp.exp(inter_log - m_t)

        qb = q_ref[pl.ds(t0, L), :]
        kb = k_ref[pl.ds(t0, L), :]
        q = qb.astype(F32)
        k = kb.astype(F32)
        v = v_ref[pl.ds(t0, L), :]

        s = (_dot_nt(qb, kb) * qscale) * dmat
        inter = _dot(qb, c_ref[...].astype(BF16)) * qscale
        num = _dot(s.astype(BF16), v) + inter_w * inter
        qn = jnp.sum(q * n_prev, axis=1, keepdims=True) * qscale
        den = jnp.sum(s, axis=1, keepdims=True) + inter_w * qn
        hh = num / jnp.maximum(jnp.abs(den), jnp.exp(-m_t))

        hn = _rms(hh, g_ref[...])
        og = og_ref[pl.ds(t0, L), :].astype(F32)
        y_ref[pl.ds(t0, L), :] = (jax.nn.sigmoid(og) * hn).astype(y_ref.dtype)

        b_last = b_col[L - 1:L, :]
        upd = b_last - b_col + i_col
        m_new = jnp.maximum(b_last + m_prev, jnp.max(upd, axis=0, keepdims=True))
        w_col = jnp.exp(upd - m_new)
        decay = jnp.exp(b_last + m_prev - m_new)
        kw = k * w_col
        c_ref[...] = decay * c_ref[...] + _dot_tn(kw.astype(BF16), v)
        n_new = decay * n_prev + jnp.sum(kw, axis=0, keepdims=True)
        return m_new, n_new

    lax.fori_loop(0, seq // L, chunk, (jnp.zeros((1, 1), F32), jnp.zeros((1, d), F32)), unroll=8)


def _mlstm(zb, zqk, zif, bias_if, norm_g, *, bsz, seq):
    d = ML_HEAD_DIM
    kmv, kmo = ZB_MV // d, ZB_MO // d
    return pl.pallas_call(
        functools.partial(_mlstm_kernel, seq=seq),
        grid=(bsz, ML_HEADS),
        in_specs=[
            pl.BlockSpec(memory_space=pltpu.SMEM),
            pl.BlockSpec((seq, d), lambda b, h: (b, h)),
            pl.BlockSpec((seq, d), lambda b, h: (b, ML_HEADS + h)),
            pl.BlockSpec((seq, d), lambda b, h: (b, kmv + h)),
            pl.BlockSpec((seq, d), lambda b, h: (b, kmo + h)),
            pl.BlockSpec((seq, LANES), lambda b, h: (b, 0)),
            pl.BlockSpec((1, d), lambda b, h: (0, h)),
        ],
        out_specs=pl.BlockSpec((seq, d), lambda b, h: (b, h)),
        out_shape=jax.ShapeDtypeStruct((bsz * seq, ML_WIDTH), BF16),
        scratch_shapes=[pltpu.VMEM((d, d), F32)],
        compiler_params=_params("parallel", "parallel"),
        name="mlstm",
    )(bias_if, zqk, zqk, zb, zb, zif, norm_g)


def _merge_kernel(ya_ref, yb_ref, ga_ref, gb_ref, x_ref, wpa_ref, wpb_ref, wo_ref, wup_ref, wdn_ref,
                  o_ref, wup_bf_ref, wdn_bf_ref):
    pa = _dot(ya_ref[...], wpa_ref[...])
    pb = _dot(yb_ref[...], wpb_ref[...])
    merged = (jax.nn.sigmoid(ga_ref[...].astype(F32)) * pa
              + jax.nn.sigmoid(gb_ref[...].astype(F32)) * pb)
    o_ref[...] = x_ref[...] + _dot(merged.astype(BF16), wo_ref[...])
    wup_bf_ref[...] = wup_ref[...].astype(BF16)
    wdn_bf_ref[...] = wdn_ref[...].astype(BF16)


def _resident(shape):
    return pl.BlockSpec(shape, lambda *_: (0,) * len(shape), pipeline_mode=pl.Buffered(1))


def _merge(ya, yb, zb, x2, w_pa, w_pb, w_o, w_up, w_down, *, tm=256):
    t = x2.shape[0]
    steps = t // tm
    fw = D_FF // steps
    assert fw % LANES == 0
    kga, kgb = ZB_GA // D_MODEL, ZB_GB // D_MODEL
    return pl.pallas_call(
        _merge_kernel,
        grid=(steps,),
        in_specs=[
            pl.BlockSpec((tm, DA_WIDTH), lambda i: (i, 0)),
            pl.BlockSpec((tm, ML_WIDTH), lambda i: (i, 0)),
            pl.BlockSpec((tm, D_MODEL), lambda i: (i, kga)),
            pl.BlockSpec((tm, D_MODEL), lambda i: (i, kgb)),
            pl.BlockSpec((tm, D_MODEL), lambda i: (i, 0)),
            _resident((DA_WIDTH, D_MODEL)),
            _resident((ML_WIDTH, D_MODEL)),
            _resident((D_MODEL, D_MODEL)),
            pl.BlockSpec((D_MODEL, fw), lambda i: (0, i)),
            pl.BlockSpec((fw, D_MODEL), lambda i: (i, 0)),
        ],
        out_specs=[
            pl.BlockSpec((tm, D_MODEL), lambda i: (i, 0)),
            pl.BlockSpec((D_MODEL, fw), lambda i: (0, i)),
            pl.BlockSpec((fw, D_MODEL), lambda i: (i, 0)),
        ],
        out_shape=[
            jax.ShapeDtypeStruct((t, D_MODEL), F32),
            jax.ShapeDtypeStruct((D_MODEL, D_FF), BF16),
            jax.ShapeDtypeStruct((D_FF, D_MODEL), BF16),
        ],
        compiler_params=_params("parallel"),
        name="merge",
    )(ya, yb, zb, zb, x2, w_pa, w_pb, w_o, w_up, w_down)


def _mlp_kernel(x_ref, g_ref, wup_ref, wdn_ref, o_ref, h_ref):
    f = pl.program_id(1)

    @pl.when(f == 0)
    def _():
        x = x_ref[...]
        h_ref[...] = _rms(x, g_ref[...]).astype(BF16)
        o_ref[...] = x

    u = _dot(h_ref[...], wup_ref[...])
    a = jnp.square(jnp.maximum(u, 0.0)).astype(BF16)
    o_ref[...] += _dot(a, wdn_ref[...])


def _mlp(x1, g_mlp, w_up, w_down, *, tm=1024, tf=1024):
    t = x1.shape[0]
    return pl.pallas_call(
        _mlp_kernel,
        grid=(t // tm, D_FF // tf),
        in_specs=[
            pl.BlockSpec((tm, D_MODEL), lambda i, f: (i, 0)),
            pl.BlockSpec((1, D_MODEL), lambda i, f: (0, 0)),
            pl.BlockSpec((D_MODEL, tf), lambda i, f: (0, f)),
            pl.BlockSpec((tf, D_MODEL), lambda i, f: (f, 0)),
        ],
        out_specs=pl.BlockSpec((tm, D_MODEL), lambda i, f: (i, 0)),
        out_shape=jax.ShapeDtypeStruct((t, D_MODEL), F32),
        scratch_shapes=[pltpu.VMEM((tm, D_MODEL), BF16)],
        compiler_params=_params("parallel", "arbitrary", vmem_limit_bytes=MLP_VMEM_LIMIT_BYTES),
        name="mlp",
    )(x1, g_mlp, w_up, w_down)


def _ple_kernel(x_ref, p_ref, gp_ref, wg_ref, wp_ref, gf_ref, o_ref):
    x = x_ref[...]
    gate = jax.nn.sigmoid(_dot(_rms(x, gp_ref[...]).astype(BF16), wg_ref[...]))
    proj = _dot(p_ref[...].astype(BF16), wp_ref[...])
    o_ref[...] = _rms(x + gate * proj, gf_ref[...])


def _ple(x2, p2, g_ple, w_gate, w_proj, g_final, *, tm=512):
    t = x2.shape[0]
    return pl.pallas_call(
        _ple_kernel,
        grid=(t // tm,),
        in_specs=[
            pl.BlockSpec((tm, D_MODEL), lambda i: (i, 0)),
            pl.BlockSpec((tm, PLE_DIM), lambda i: (i, 0)),
            pl.BlockSpec((1, D_MODEL), lambda i: (0, 0)),
            _resident((D_MODEL, D_MODEL)),
            _resident((PLE_DIM, D_MODEL)),
            pl.BlockSpec((1, D_MODEL), lambda i: (0, 0)),
        ],
        out_specs=pl.BlockSpec((tm, D_MODEL), lambda i: (i, 0)),
        out_shape=jax.ShapeDtypeStruct((t, D_MODEL), F32),
        compiler_params=_params("parallel"),
        name="ple_final",
    )(x2, p2, g_ple, w_gate, w_proj, g_final)


def kernel(x, p, g_mix, w_in, conv_w, conv_b, b_i, b_f, lam_q1, lam_k1, lam_q2, lam_k2,
           da_sub_g, ml_norm_g, w_pa, w_pb, w_o, g_mlp, w_up, w_down, g_ple, w_ple_gate,
           w_ple_proj, g_final):
    bsz, seq, _ = x.shape
    t = bsz * seq
    assert w_in.shape[0] == 1, "single-layer block only"
    lambda_init = 0.8 - 0.6 * math.exp(-0.3 * 0)
    x2 = x.reshape(t, D_MODEL)
    row = lambda v: v.reshape(1, -1).astype(F32)

    w_a = jnp.swapaxes(w_in[0], 0, 1).astype(BF16)
    c_if, c_gate = 7 * 1024, 7 * 1024 + 2 * ML_HEADS
    w_g = w_a[c_gate:]
    w_if = jnp.pad(w_a[c_if:c_gate], ((0, LANES - 2 * ML_HEADS), (0, 0)))
    bias_if = jnp.concatenate([b_i[0], b_f[0]]).astype(F32)
    lam4 = jnp.stack([lam_q1[0], lam_k1[0], lam_q2[0], lam_k2[0]]).astype(F32)

    zb, zqk, zif = _in_proj(x2, row(g_mix[0]), w_a, w_g, w_if, conv_w[0].astype(F32),
                            row(conv_b[0]), seq=seq)
    ya, w_pa_bf, w_pb_bf, w_o_bf, w_pg_bf = _attention(
        zb, lam4, da_sub_g[0].reshape(DA_V_DIM, 1).astype(F32),
        [w_pa[0], w_pb[0], w_o[0], w_ple_gate[0]], bsz=bsz, seq=seq, lambda_init=lambda_init)
    yb = _mlstm(zb, zqk, zif, bias_if, row(ml_norm_g[0]), bsz=bsz, seq=seq)
    x2, w_up_bf, w_down_bf = _merge(ya, yb, zb, x2, w_pa_bf, w_pb_bf, w_o_bf, w_up[0], w_down[0])
    x2 = _mlp(x2, row(g_mlp[0]), w_up_bf, w_down_bf)
    x2 = _ple(x2, p[0].reshape(t, PLE_DIM), row(g_ple[0]), w_pg_bf,
              w_ple_proj[0].astype(BF16), row(g_final))
    return x2.reshape(bsz, seq, D_MODEL)
```

```python
import functools
import math

import jax
import jax.numpy as jnp
from jax import lax
from jax.experimental import pallas as pl
from jax.experimental.pallas import tpu as pltpu

F32 = jnp.float32
BF16 = jnp.bfloat16

D_MODEL = 2048
CHUNK = 64
EPS = 1e-6
DA_HEADS = 8
DA_V_DIM = 128
DA_QK_DIM = 64
DA_WIDTH = DA_HEADS * DA_V_DIM
ML_HEADS = 4
ML_HEAD_DIM = 256
ML_WIDTH = ML_HEADS * ML_HEAD_DIM
CONV_K = 4
D_FF = 4 * D_MODEL
PLE_DIM = 256
NEG_INF = -1e30

LANES = 128
SUBLANES = 8
VMEM_LIMIT_BYTES = 56 * 1024 * 1024
MLP_VMEM_LIMIT_BYTES = 60 * 1024 * 1024

ZB_GA, ZB_GB, ZB_AQ, ZB_AK, ZB_AV, ZB_MV, ZB_MO = 0, 2048, 4096, 5120, 6144, 7168, 8192
ZB_COLS = 9216
ZQK_COLS = 2 * ML_WIDTH
ATTN_Q_SCALE = DA_QK_DIM ** -0.5 * math.log2(math.e)


def _rms(x, g):
    ms = jnp.mean(x * x, axis=-1, keepdims=True)
    return (x * lax.rsqrt(ms + EPS)) * g


def _dot(a, b):
    return jnp.dot(a, b, preferred_element_type=F32)


def _dot_nt(a, b):
    return lax.dot_general(a, b, (((1,), (1,)), ((), ())), preferred_element_type=F32)


def _dot_tn(a, b):
    return lax.dot_general(a, b, (((0,), (0,)), ((), ())), preferred_element_type=F32)


def _params(*sem, vmem_limit_bytes=VMEM_LIMIT_BYTES):
    return pltpu.CompilerParams(dimension_semantics=sem, vmem_limit_bytes=vmem_limit_bytes)


IN_TN = 1024
IN_NB_GATE = 2 * D_MODEL // IN_TN
IN_J_AQ = IN_NB_GATE
IN_NB_BF16 = ZB_COLS // IN_TN


def _in_proj_kernel(x_ref, g_ref, wa_ref, wg_ref, wif_ref, cw_ref, cb_ref, zb_ref, zqk_ref, zif_ref,
                    h_ref, tail_ref, *, tiles_per_seq):
    i = pl.program_id(0)
    j = pl.program_id(1)
    tm = x_ref.shape[0]

    @pl.when(j == 0)
    def _():
        hb = _rms(x_ref[...], g_ref[...]).astype(BF16)
        h_ref[...] = hb
        zif_ref[...] = _dot_nt(hb, wif_ref[...])

    @pl.when(j < IN_NB_GATE)
    def _():
        zb_ref[...] = _dot_nt(h_ref[...], wg_ref[...]).astype(BF16)

    @pl.when(j == IN_J_AQ)
    def _():
        zb_ref[...] = (_dot_nt(h_ref[...], wa_ref[...]) * ATTN_Q_SCALE).astype(BF16)

    @pl.when((j > IN_J_AQ) & (j < IN_NB_BF16))
    def _():
        zb_ref[...] = _dot_nt(h_ref[...], wa_ref[...]).astype(BF16)

    @pl.when(j >= IN_NB_BF16)
    def _():
        jj = j - IN_NB_BF16
        z = _dot_nt(h_ref[...], wa_ref[...])
        prev = jnp.where(i % tiles_per_seq == 0, 0.0, tail_ref[jj])
        tail_ref[jj] = z[tm - SUBLANES:, :]
        xw = jnp.concatenate([prev, z], axis=0)
        y = cb_ref[...]
        for tap in range(CONV_K):
            off = SUBLANES - (CONV_K - 1) + tap
            y = y + cw_ref[tap:tap + 1, :] * xw[off:off + tm, :]
        zqk_ref[...] = (y * jax.nn.sigmoid(y)).astype(BF16)


def _in_proj(x2, g_mix, w_a, w_g, w_if, conv_w, conv_b, *, seq, tm=1024):
    t = x2.shape[0]
    tn = IN_TN
    nb = IN_NB_BF16 + ZQK_COLS // tn

    def wa_blk(i, j):
        ja = j - IN_J_AQ
        return jnp.where(ja <= 2, jnp.maximum(ja, 0), jnp.where(ja <= 4, ja + 2, ja - 2)), 0

    qk_blk = lambda i, j: (0, jnp.maximum(j - IN_NB_BF16, 0))
    return pl.pallas_call(
        functools.partial(_in_proj_kernel, tiles_per_seq=seq // tm),
        grid=(t // tm, nb),
        in_specs=[
            pl.BlockSpec((tm, D_MODEL), lambda i, j: (i, 0)),
            pl.BlockSpec((1, D_MODEL), lambda i, j: (0, 0)),
            pl.BlockSpec((tn, D_MODEL), wa_blk),
            pl.BlockSpec((tn, D_MODEL), lambda i, j: (jnp.minimum(j, IN_NB_GATE - 1), 0)),
            pl.BlockSpec((LANES, D_MODEL), lambda i, j: (0, 0)),
            pl.BlockSpec((CONV_K, tn), qk_blk),
            pl.BlockSpec((1, tn), qk_blk),
        ],
        out_specs=[
            pl.BlockSpec((tm, tn), lambda i, j: (i, jnp.minimum(j, IN_NB_BF16 - 1))),
            pl.BlockSpec((tm, tn), lambda i, j: (i, jnp.maximum(j - IN_NB_BF16, 0))),
            pl.BlockSpec((tm, LANES), lambda i, j: (i, 0)),
        ],
        out_shape=[
            jax.ShapeDtypeStruct((t, ZB_COLS), BF16),
            jax.ShapeDtypeStruct((t, ZQK_COLS), BF16),
            jax.ShapeDtypeStruct((t, LANES), F32),
        ],
        scratch_shapes=[
            pltpu.VMEM((tm, D_MODEL), BF16),
            pltpu.VMEM((ZQK_COLS // tn, SUBLANES, tn), F32),
        ],
        compiler_params=_params("arbitrary", "arbitrary"),
        name="in_proj",
    )(x2, g_mix, w_a, w_g, w_if, conv_w, conv_b)


def _attn_kernel(lam_ref, q_ref, k_ref, v_ref, g_ref, *rest, tq, tk, hg, lambda_init, n_cast):
    cast_in, (o_ref, *cast_out) = rest[:n_cast], rest[n_cast:2 * n_cast + 1]
    vt_ref, s0_ref, s1_ref, acc_ref = rest[2 * n_cast + 1:]
    for src, dst in zip(cast_in, cast_out):
        dst[...] = src[...].astype(BF16)
    seq = q_ref.shape[0]
    dv = DA_V_DIM
    heads = range(hg)
    cols = lambda h: slice(h * dv, (h + 1) * dv)

    for h in heads:
        for kb in range(seq // tk):
            vt_ref[h, kb, :dv, :] = v_ref[kb * tk:(kb + 1) * tk, cols(h)].astype(F32).T.astype(BF16)
            vt_ref[h, kb, dv:, :] = jnp.ones((vt_ref.shape[2] - dv, tk), BF16)

    lq = lam_ref[...]
    t1 = jnp.sum(lq[0:1, :] * lq[1:2, :], axis=-1, keepdims=True)
    t2 = jnp.sum(lq[2:3, :] * lq[3:4, :], axis=-1, keepdims=True)
    lam = jnp.exp(t1) - jnp.exp(t2) + lambda_init
    s_bufs = (s0_ref, s1_ref)

    for qi in range(seq // tq):
        q_maps = []
        for h in heads:
            q = q_ref[qi * tq:(qi + 1) * tq, cols(h)]
            lane = lax.broadcasted_iota(jnp.int32, q.shape, 1)
            zero = jnp.zeros_like(q)
            q_maps.append((jnp.where(lane < DA_QK_DIM, q, zero), jnp.where(lane >= DA_QK_DIM, q, zero)))
        n_kb = (qi + 1) * tq // tk
        n_full = qi * tq // tk

        def first_query(kb):
            return max(kb - n_full, 0) * tk

        def scores(kb):
            lo = first_query(kb)
            for h in heads:
                k = k_ref[kb * tk:(kb + 1) * tk, cols(h)]
                for mp in range(2):
                    s_bufs[kb % 2][2 * h + mp, :, :tq - lo] = _dot_nt(k, q_maps[h][mp][lo:, :])

        m_run = [jnp.full((1, tq), NEG_INF, F32)] * (2 * hg)
        scores(0)
        for kb in range(n_kb):
            if kb + 1 < n_kb:
                scores(kb + 1)
            lo = first_query(kb)
            w = tq - lo
            for mp in range(2 * hg):
                s = s_bufs[kb % 2][mp, :, :w]
                if kb >= n_full:
                    key = lax.broadcasted_iota(jnp.int32, (tk, w), 0)
                    qry = lax.broadcasted_iota(jnp.int32, (tk, w), 1)
                    s = jnp.where((key // CHUNK) <= (qry // CHUNK), s, NEG_INF)
                blk_max = jnp.max(s.reshape(tk // SUBLANES, SUBLANES, w).max(axis=0),
                                  axis=0, keepdims=True)
                m_prev = m_run[mp][:, lo:]
                m_next = jnp.maximum(m_prev, blk_max)
                p = jnp.exp2(s - m_next).astype(BF16)
                pv = _dot(vt_ref[mp // 2, kb], p)
                if kb == 0:
                    acc_ref[mp] = pv
                else:
                    acc_ref[mp, :, lo:] = jnp.exp2(m_prev - m_next) * acc_ref[mp, :, lo:] + pv
                m_run[mp] = jnp.concatenate([m_run[mp][:, :lo], m_next], axis=1) if lo else m_next

        for h in heads:
            a1 = acc_ref[2 * h]
            a2 = acc_ref[2 * h + 1]
            o = a1[:dv] / a1[dv:dv + 1] - lam * (a2[:dv] / a2[dv:dv + 1])
            ms = jnp.mean(o * o, axis=0, keepdims=True)
            y = ((o * lax.rsqrt(ms + EPS)) * g_ref[...]) * (1.0 - lambda_init)
            o_ref[qi * tq:(qi + 1) * tq, cols(h)] = y.T.astype(o_ref.dtype)


def _attention(zb, lam4, sub_g_col, cast_ws, *, bsz, seq, lambda_init, tq=1024, tk=256, hg=2):
    gw = hg * DA_V_DIM
    kq, kk, kv = ZB_AQ // gw, ZB_AK // gw, ZB_AV // gw
    vt_rows = DA_V_DIM + 16
    groups = DA_HEADS // hg
    steps = bsz * groups
    slab = lambda b, g: (b * groups + g, 0)
    cast_specs = [pl.BlockSpec((w.shape[0] // steps, w.shape[1]), slab) for w in cast_ws]
    assert all(w.shape[0] % (16 * steps) == 0 for w in cast_ws)
    return pl.pallas_call(
        functools.partial(_attn_kernel, tq=tq, tk=tk, hg=hg, lambda_init=lambda_init,
                          n_cast=len(cast_ws)),
        grid=(bsz, groups),
        in_specs=[
            pl.BlockSpec((4, DA_QK_DIM), lambda b, g: (0, 0)),
            pl.BlockSpec((seq, gw), lambda b, g: (b, kq + g)),
            pl.BlockSpec((seq, gw), lambda b, g: (b, kk + g)),
            pl.BlockSpec((seq, gw), lambda b, g: (b, kv + g)),
            pl.BlockSpec((DA_V_DIM, 1), lambda b, g: (0, 0)),
            *cast_specs,
        ],
        out_specs=[pl.BlockSpec((seq, gw), lambda b, g: (b, g)), *cast_specs],
        out_shape=[jax.ShapeDtypeStruct((bsz * seq, DA_WIDTH), BF16),
                   *[jax.ShapeDtypeStruct(w.shape, BF16) for w in cast_ws]],
        scratch_shapes=[
            pltpu.VMEM((hg, seq // tk, vt_rows, tk), BF16),
            pltpu.VMEM((2 * hg, tk, tq), F32),
            pltpu.VMEM((2 * hg, tk, tq), F32),
            pltpu.VMEM((2 * hg, vt_rows, tq), F32),
        ],
        compiler_params=_params("parallel", "parallel"),
        name="diff_attn",
    )(lam4, zb, zb, zb, sub_g_col, *cast_ws)


def _log_sigmoid(x):
    return jnp.minimum(x, 0.0) - jnp.log(1.0 + jnp.exp(-jnp.abs(x)))


def _mlstm_kernel(bias_ref, q_ref, k_ref, v_ref, og_ref, zif_ref, g_ref, y_ref, c_ref, *, seq):
    L = CHUNK
    d = ML_HEAD_DIM
    h = pl.program_id(1)
    b_i = bias_ref[h]
    b_f = bias_ref[ML_HEADS + h]
    c_ref[...] = jnp.zeros(c_ref.shape, F32)

    lane = lax.broadcasted_iota(jnp.int32, (L, LANES), 1)
    sel_i = lane == h
    sel_f = lane == h + ML_HEADS
    row = lax.broadcasted_iota(jnp.int32, (L, L), 0)
    col = lax.broadcasted_iota(jnp.int32, (L, L), 1)
    causal = col <= row
    eye = col == row
    qscale = d ** -0.5

    def chunk(c, carry):
        m_prev, n_prev = carry
        t0 = pl.multiple_of(c * L, L)
        z = zif_ref[pl.ds(t0, L), :]
        i_col = jnp.sum(jnp.where(sel_i, z, 0.0), axis=1, keepdims=True) + b_i
        f_col = _log_sigmoid(jnp.sum(jnp.where(sel_f, z, 0.0), axis=1, keepdims=True) + b_f)
        f_mat = jnp.broadcast_to(f_col, (L, L))
        i_mat = jnp.broadcast_to(i_col, (L, L))
        b_row = jnp.sum(jnp.where(row <= col, f_mat, 0.0), axis=0, keepdims=True)
        i_row = jnp.sum(jnp.where(eye, i_mat, 0.0), axis=0, keepdims=True)
        b_col = jnp.sum(jnp.where(eye, jnp.broadcast_to(b_row, (L, L)), 0.0), axis=1, keepdims=True)

        dlog = jnp.where(causal, b_col - b_row + i_row, NEG_INF)
        inter_log = b_col + m_prev
        m_t = jnp.maximum(inter_log, jnp.max(dlog, axis=1, keepdims=True))
        dmat = jnp.exp(dlog - m_t)
        inter_w = jnp.exp(inter_log - m_t)

        qb = q_ref[pl.ds(t0, L), :]
        kb = k_ref[pl.ds(t0, L), :]
        q = qb.astype(F32)
        k = kb.astype(F32)
        v = v_ref[pl.ds(t0, L), :]

        s = (_dot_nt(qb, kb) * qscale) * dmat
        inter = _dot(qb, c_ref[...].astype(BF16)) * qscale
        num = _dot(s.astype(BF16), v) + inter_w * inter
        qn = jnp.sum(q * n_prev, axis=1, keepdims=True) * qscale
        den = jnp.sum(s, axis=1, keepdims=True) + inter_w * qn
        hh = num / jnp.maximum(jnp.abs(den), jnp.exp(-m_t))

        hn = _rms(hh, g_ref[...])
        og = og_ref[pl.ds(t0, L), :].astype(F32)
        y_ref[pl.ds(t0, L), :] = (jax.nn.sigmoid(og) * hn).astype(y_ref.dtype)

        b_last = b_col[L - 1:L, :]
        upd = b_last - b_col + i_col
        m_new = jnp.maximum(b_last + m_prev, jnp.max(upd, axis=0, keepdims=True))
        w_col = jnp.exp(upd - m_new)
        decay = jnp.exp(b_last + m_prev - m_new)
        kw = k * w_col
        c_ref[...] = decay * c_ref[...] + _dot_tn(kw.astype(BF16), v)
        n_new = decay * n_prev + jnp.sum(kw, axis=0, keepdims=True)
        return m_new, n_new

    lax.fori_loop(0, seq // L, chunk, (jnp.zeros((1, 1), F32), jnp.zeros((1, d), F32)), unroll=16)


def _mlstm(zb, zqk, zif, bias_if, norm_g, *, bsz, seq):
    d = ML_HEAD_DIM
    kmv, kmo = ZB_MV // d, ZB_MO // d
    return pl.pallas_call(
        functools.partial(_mlstm_kernel, seq=seq),
        grid=(bsz, ML_HEADS),
        in_specs=[
            pl.BlockSpec(memory_space=pltpu.SMEM),
            pl.BlockSpec((seq, d), lambda b, h: (b, h)),
            pl.BlockSpec((seq, d), lambda b, h: (b, ML_HEADS + h)),
            pl.BlockSpec((seq, d), lambda b, h: (b, kmv + h)),
            pl.BlockSpec((seq, d), lambda b, h: (b, kmo + h)),
            pl.BlockSpec((seq, LANES), lambda b, h: (b, 0)),
            pl.BlockSpec((1, d), lambda b, h: (0, h)),
        ],
        out_specs=pl.BlockSpec((seq, d), lambda b, h: (b, h)),
        out_shape=jax.ShapeDtypeStruct((bsz * seq, ML_WIDTH), BF16),
        scratch_shapes=[pltpu.VMEM((d, d), F32)],
        compiler_params=_params("parallel", "parallel"),
        name="mlstm",
    )(bias_if, zqk, zqk, zb, zb, zif, norm_g)


def _merge_kernel(ya_ref, yb_ref, ga_ref, gb_ref, x_ref, wpa_ref, wpb_ref, wo_ref, wup_ref, wdn_ref,
                  o_ref, wup_bf_ref, wdn_bf_ref):
    pa = _dot(ya_ref[...], wpa_ref[...])
    pb = _dot(yb_ref[...], wpb_ref[...])
    merged = (jax.nn.sigmoid(ga_ref[...].astype(F32)) * pa
              + jax.nn.sigmoid(gb_ref[...].astype(F32)) * pb)
    o_ref[...] = x_ref[...] + _dot(merged.astype(BF16), wo_ref[...])
    wup_bf_ref[...] = wup_ref[...].astype(BF16)
    wdn_bf_ref[...] = wdn_ref[...].astype(BF16)


def _resident(shape):
    return pl.BlockSpec(shape, lambda *_: (0,) * len(shape), pipeline_mode=pl.Buffered(1))


def _merge(ya, yb, zb, x2, w_pa, w_pb, w_o, w_up, w_down, *, tm=256):
    t = x2.shape[0]
    steps = t // tm
    fw = D_FF // steps
    assert fw % LANES == 0
    kga, kgb = ZB_GA // D_MODEL, ZB_GB // D_MODEL
    return pl.pallas_call(
        _merge_kernel,
        grid=(steps,),
        in_specs=[
            pl.BlockSpec((tm, DA_WIDTH), lambda i: (i, 0)),
            pl.BlockSpec((tm, ML_WIDTH), lambda i: (i, 0)),
            pl.BlockSpec((tm, D_MODEL), lambda i: (i, kga)),
            pl.BlockSpec((tm, D_MODEL), lambda i: (i, kgb)),
            pl.BlockSpec((tm, D_MODEL), lambda i: (i, 0)),
            _resident((DA_WIDTH, D_MODEL)),
            _resident((ML_WIDTH, D_MODEL)),
            _resident((D_MODEL, D_MODEL)),
            pl.BlockSpec((D_MODEL, fw), lambda i: (0, i)),
            pl.BlockSpec((fw, D_MODEL), lambda i: (i, 0)),
        ],
        out_specs=[
            pl.BlockSpec((tm, D_MODEL), lambda i: (i, 0)),
            pl.BlockSpec((D_MODEL, fw), lambda i: (0, i)),
            pl.BlockSpec((fw, D_MODEL), lambda i: (i, 0)),
        ],
        out_shape=[
            jax.ShapeDtypeStruct((t, D_MODEL), F32),
            jax.ShapeDtypeStruct((D_MODEL, D_FF), BF16),
            jax.ShapeDtypeStruct((D_FF, D_MODEL), BF16),
        ],
        compiler_params=_params("parallel"),
        name="merge",
    )(ya, yb, zb, zb, x2, w_pa, w_pb, w_o, w_up, w_down)


def _mlp_kernel(x_ref, g_ref, wup_ref, wdn_ref, o_ref, h_ref):
    f = pl.program_id(1)

    @pl.when(f == 0)
    def _():
        x = x_ref[...]
        h_ref[...] = _rms(x, g_ref[...]).astype(BF16)
        o_ref[...] = x

    u = _dot(h_ref[...], wup_ref[...])
    a = jnp.square(jnp.maximum(u, 0.0)).astype(BF16)
    o_ref[...] += _dot(a, wdn_ref[...])


def _mlp(x1, g_mlp, w_up, w_down, *, tm=1024, tf=1024):
    t = x1.shape[0]
    return pl.pallas_call(
        _mlp_kernel,
        grid=(t // tm, D_FF // tf),
        in_specs=[
            pl.BlockSpec((tm, D_MODEL), lambda i, f: (i, 0)),
            pl.BlockSpec((1, D_MODEL), lambda i, f: (0, 0)),
            pl.BlockSpec((D_MODEL, tf), lambda i, f: (0, f)),
            pl.BlockSpec((tf, D_MODEL), lambda i, f: (f, 0)),
        ],
        out_specs=pl.BlockSpec((tm, D_MODEL), lambda i, f: (i, 0)),
        out_shape=jax.ShapeDtypeStruct((t, D_MODEL), F32),
        scratch_shapes=[pltpu.VMEM((tm, D_MODEL), BF16)],
        compiler_params=_params("parallel", "arbitrary", vmem_limit_bytes=MLP_VMEM_LIMIT_BYTES),
        name="mlp",
    )(x1, g_mlp, w_up, w_down)


def _ple_kernel(x_ref, p_ref, gp_ref, wg_ref, wp_ref, gf_ref, o_ref):
    x = x_ref[...]
    gate = jax.nn.sigmoid(_dot(_rms(x, gp_ref[...]).astype(BF16), wg_ref[...]))
    proj = _dot(p_ref[...].astype(BF16), wp_ref[...])
    o_ref[...] = _rms(x + gate * proj, gf_ref[...])


def _ple(x2, p2, g_ple, w_gate, w_proj, g_final, *, tm=512):
    t = x2.shape[0]
    return pl.pallas_call(
        _ple_kernel,
        grid=(t // tm,),
        in_specs=[
            pl.BlockSpec((tm, D_MODEL), lambda i: (i, 0)),
            pl.BlockSpec((tm, PLE_DIM), lambda i: (i, 0)),
            pl.BlockSpec((1, D_MODEL), lambda i: (0, 0)),
            _resident((D_MODEL, D_MODEL)),
            _resident((PLE_DIM, D_MODEL)),
            pl.BlockSpec((1, D_MODEL), lambda i: (0, 0)),
        ],
        out_specs=pl.BlockSpec((tm, D_MODEL), lambda i: (i, 0)),
        out_shape=jax.ShapeDtypeStruct((t, D_MODEL), F32),
        compiler_params=_params("parallel"),
        name="ple_final",
    )(x2, p2, g_ple, w_gate, w_proj, g_final)


def kernel(x, p, g_mix, w_in, conv_w, conv_b, b_i, b_f, lam_q1, lam_k1, lam_q2, lam_k2,
           da_sub_g, ml_norm_g, w_pa, w_pb, w_o, g_mlp, w_up, w_down, g_ple, w_ple_gate,
           w_ple_proj, g_final):
    bsz, seq, _ = x.shape
    t = bsz * seq
    assert w_in.shape[0] == 1, "single-layer block only"
    lambda_init = 0.8 - 0.6 * math.exp(-0.3 * 0)
    x2 = x.reshape(t, D_MODEL)
    row = lambda v: v.reshape(1, -1).astype(F32)

    w_a = jnp.swapaxes(w_in[0], 0, 1).astype(BF16)
    c_if, c_gate = 7 * 1024, 7 * 1024 + 2 * ML_HEADS
    w_g = w_a[c_gate:]
    w_if = jnp.pad(w_a[c_if:c_gate], ((0, LANES - 2 * ML_HEADS), (0, 0)))
    bias_if = jnp.concatenate([b_i[0], b_f[0]]).astype(F32)
    lam4 = jnp.stack([lam_q1[0], lam_k1[0], lam_q2[0], lam_k2[0]]).astype(F32)

    zb, zqk, zif = _in_proj(x2, row(g_mix[0]), w_a, w_g, w_if, conv_w[0].astype(F32),
                            row(conv_b[0]), seq=seq)
    ya, w_pa_bf, w_pb_bf, w_o_bf, w_pg_bf = _attention(
        zb, lam4, da_sub_g[0].reshape(DA_V_DIM, 1).astype(F32),
        [w_pa[0], w_pb[0], w_o[0], w_ple_gate[0]], bsz=bsz, seq=seq, lambda_init=lambda_init)
    yb = _mlstm(zb, zqk, zif, bias_if, row(ml_norm_g[0]), bsz=bsz, seq=seq)
    x2, w_up_bf, w_down_bf = _merge(ya, yb, zb, x2, w_pa_bf, w_pb_bf, w_o_bf, w_up[0], w_down[0])
    x2 = _mlp(x2, row(g_mlp[0]), w_up_bf, w_down_bf)
    x2 = _ple(x2, p[0].reshape(t, PLE_DIM), row(g_ple[0]), w_pg_bf,
              w_ple_proj[0].astype(BF16), row(g_final))
    return x2.reshape(bsz, seq, D_MODEL)
```

```python
import functools
import math

import jax
import jax.numpy as jnp
from jax import lax
from jax.experimental import pallas as pl
from jax.experimental.pallas import tpu as pltpu

F32 = jnp.float32
BF16 = jnp.bfloat16

D_MODEL = 2048
CHUNK = 64
EPS = 1e-6
DA_HEADS = 8
DA_V_DIM = 128
DA_QK_DIM = 64
DA_WIDTH = DA_HEADS * DA_V_DIM
ML_HEADS = 4
ML_HEAD_DIM = 256
ML_WIDTH = ML_HEADS * ML_HEAD_DIM
CONV_K = 4
D_FF = 4 * D_MODEL
PLE_DIM = 256
NEG_INF = -1e30

LANES = 128
SUBLANES = 8
VMEM_LIMIT_BYTES = 56 * 1024 * 1024
MLP_VMEM_LIMIT_BYTES = 60 * 1024 * 1024

ZB_GA, ZB_GB, ZB_AQ, ZB_AK, ZB_AV, ZB_MV, ZB_MO = 0, 2048, 4096, 5120, 6144, 7168, 8192
ZB_COLS = 9216
ZQK_COLS = 2 * ML_WIDTH
ATTN_Q_SCALE = DA_QK_DIM ** -0.5 * math.log2(math.e)


def _rms(x, g):
    ms = jnp.mean(x * x, axis=-1, keepdims=True)
    return (x * lax.rsqrt(ms + EPS)) * g


def _dot(a, b):
    return jnp.dot(a, b, preferred_element_type=F32)


def _dot_nt(a, b):
    return lax.dot_general(a, b, (((1,), (1,)), ((), ())), preferred_element_type=F32)


def _dot_tn(a, b):
    return lax.dot_general(a, b, (((0,), (0,)), ((), ())), preferred_element_type=F32)


def _params(*sem, vmem_limit_bytes=VMEM_LIMIT_BYTES):
    return pltpu.CompilerParams(dimension_semantics=sem, vmem_limit_bytes=vmem_limit_bytes)


IN_TN = 1024
IN_NB_GATE = 2 * D_MODEL // IN_TN
IN_J_AQ = IN_NB_GATE
IN_NB_BF16 = ZB_COLS // IN_TN


def _in_proj_kernel(x_ref, g_ref, wa_ref, wg_ref, wif_ref, cw_ref, cb_ref, zb_ref, zqk_ref, zif_ref,
                    h_ref, tail_ref, *, tiles_per_seq):
    i = pl.program_id(0)
    j = pl.program_id(1)
    tm = x_ref.shape[0]

    @pl.when(j == 0)
    def _():
        hb = _rms(x_ref[...], g_ref[...]).astype(BF16)
        h_ref[...] = hb
        zif_ref[...] = _dot_nt(hb, wif_ref[...])

    @pl.when(j < IN_NB_GATE)
    def _():
        zb_ref[...] = _dot_nt(h_ref[...], wg_ref[...]).astype(BF16)

    @pl.when(j == IN_J_AQ)
    def _():
        zb_ref[...] = (_dot_nt(h_ref[...], wa_ref[...]) * ATTN_Q_SCALE).astype(BF16)

    @pl.when((j > IN_J_AQ) & (j < IN_NB_BF16))
    def _():
        zb_ref[...] = _dot_nt(h_ref[...], wa_ref[...]).astype(BF16)

    @pl.when(j >= IN_NB_BF16)
    def _():
        jj = j - IN_NB_BF16
        z = _dot_nt(h_ref[...], wa_ref[...])
        prev = jnp.where(i % tiles_per_seq == 0, 0.0, tail_ref[jj])
        tail_ref[jj] = z[tm - SUBLANES:, :]
        xw = jnp.concatenate([prev, z], axis=0)
        y = cb_ref[...]
        for tap in range(CONV_K):
            off = SUBLANES - (CONV_K - 1) + tap
            y = y + cw_ref[tap:tap + 1, :] * xw[off:off + tm, :]
        zqk_ref[...] = (y * jax.nn.sigmoid(y)).astype(BF16)


def _in_proj(x2, g_mix, w_a, w_g, w_if, conv_w, conv_b, *, seq, tm=1024):
    t = x2.shape[0]
    tn = IN_TN
    nb = IN_NB_BF16 + ZQK_COLS // tn

    def wa_blk(i, j):
        ja = j - IN_J_AQ
        return jnp.where(ja <= 2, jnp.maximum(ja, 0), jnp.where(ja <= 4, ja + 2, ja - 2)), 0

    qk_blk = lambda i, j: (0, jnp.maximum(j - IN_NB_BF16, 0))
    return pl.pallas_call(
        functools.partial(_in_proj_kernel, tiles_per_seq=seq // tm),
        grid=(t // tm, nb),
        in_specs=[
            pl.BlockSpec((tm, D_MODEL), lambda i, j: (i, 0)),
            pl.BlockSpec((1, D_MODEL), lambda i, j: (0, 0)),
            pl.BlockSpec((tn, D_MODEL), wa_blk),
            pl.BlockSpec((tn, D_MODEL), lambda i, j: (jnp.minimum(j, IN_NB_GATE - 1), 0)),
            pl.BlockSpec((LANES, D_MODEL), lambda i, j: (0, 0)),
            pl.BlockSpec((CONV_K, tn), qk_blk),
            pl.BlockSpec((1, tn), qk_blk),
        ],
        out_specs=[
            pl.BlockSpec((tm, tn), lambda i, j: (i, jnp.minimum(j, IN_NB_BF16 - 1))),
            pl.BlockSpec((tm, tn), lambda i, j: (i, jnp.maximum(j - IN_NB_BF16, 0))),
            pl.BlockSpec((tm, LANES), lambda i, j: (i, 0)),
        ],
        out_shape=[
            jax.ShapeDtypeStruct((t, ZB_COLS), BF16),
            jax.ShapeDtypeStruct((t, ZQK_COLS), BF16),
            jax.ShapeDtypeStruct((t, LANES), F32),
        ],
        scratch_shapes=[
            pltpu.VMEM((tm, D_MODEL), BF16),
            pltpu.VMEM((ZQK_COLS // tn, SUBLANES, tn), F32),
        ],
        compiler_params=_params("arbitrary", "arbitrary"),
        name="in_proj",
    )(x2, g_mix, w_a, w_g, w_if, conv_w, conv_b)


def _attn_kernel(lam_ref, q_ref, k_ref, v_ref, g_ref, *rest, tq, tk, hg, lambda_init, n_cast):
    cast_in, (o_ref, *cast_out) = rest[:n_cast], rest[n_cast:2 * n_cast + 1]
    vt_ref, s0_ref, s1_ref, acc_ref = rest[2 * n_cast + 1:]
    for src, dst in zip(cast_in, cast_out):
        dst[...] = src[...].astype(BF16)
    seq = q_ref.shape[0]
    dv = DA_V_DIM
    heads = range(hg)
    cols = lambda h: slice(h * dv, (h + 1) * dv)

    for h in heads:
        for kb in range(seq // tk):
            vt_ref[h, kb, :dv, :] = v_ref[kb * tk:(kb + 1) * tk, cols(h)].astype(F32).T.astype(BF16)
            vt_ref[h, kb, dv:, :] = jnp.ones((vt_ref.shape[2] - dv, tk), BF16)

    lq = lam_ref[...]
    t1 = jnp.sum(lq[0:1, :] * lq[1:2, :], axis=-1, keepdims=True)
    t2 = jnp.sum(lq[2:3, :] * lq[3:4, :], axis=-1, keepdims=True)
    lam = jnp.exp(t1) - jnp.exp(t2) + lambda_init
    s_bufs = (s0_ref, s1_ref)

    for qi in range(seq // tq):
        q_maps = []
        for h in heads:
            q = q_ref[qi * tq:(qi + 1) * tq, cols(h)]
            lane = lax.broadcasted_iota(jnp.int32, q.shape, 1)
            zero = jnp.zeros_like(q)
            q_maps.append((jnp.where(lane < DA_QK_DIM, q, zero), jnp.where(lane >= DA_QK_DIM, q, zero)))
        n_kb = (qi + 1) * tq // tk
        n_full = qi * tq // tk

        def first_query(kb):
            return max(kb - n_full, 0) * tk

        def scores(kb):
            lo = first_query(kb)
            for h in heads:
                k = k_ref[kb * tk:(kb + 1) * tk, cols(h)]
                for mp in range(2):
                    s_bufs[kb % 2][2 * h + mp, :, :tq - lo] = _dot_nt(k, q_maps[h][mp][lo:, :])

        m_run = [jnp.full((1, tq), NEG_INF, F32)] * (2 * hg)
        scores(0)
        for kb in range(n_kb):
            if kb + 1 < n_kb:
                scores(kb + 1)
            lo = first_query(kb)
            w = tq - lo
            for mp in range(2 * hg):
                s = s_bufs[kb % 2][mp, :, :w]
                if kb >= n_full:
                    key = lax.broadcasted_iota(jnp.int32, (tk, w), 0)
                    qry = lax.broadcasted_iota(jnp.int32, (tk, w), 1)
                    s = jnp.where((key // CHUNK) <= (qry // CHUNK), s, NEG_INF)
                blk_max = jnp.max(s.reshape(tk // SUBLANES, SUBLANES, w).max(axis=0),
                                  axis=0, keepdims=True)
                m_prev = m_run[mp][:, lo:]
                m_next = jnp.maximum(m_prev, blk_max)
                p = jnp.exp2(s - m_next).astype(BF16)
                pv = _dot(vt_ref[mp // 2, kb], p)
                if kb == 0:
                    acc_ref[mp] = pv
                else:
                    acc_ref[mp, :, lo:] = jnp.exp2(m_prev - m_next) * acc_ref[mp, :, lo:] + pv
                m_run[mp] = jnp.concatenate([m_run[mp][:, :lo], m_next], axis=1) if lo else m_next

        for h in heads:
            a1 = acc_ref[2 * h]
            a2 = acc_ref[2 * h + 1]
            o = a1[:dv] / a1[dv:dv + 1] - lam * (a2[:dv] / a2[dv:dv + 1])
            ms = jnp.mean(o * o, axis=0, keepdims=True)
            y = ((o * lax.rsqrt(ms + EPS)) * g_ref[...]) * (1.0 - lambda_init)
            o_ref[qi * tq:(qi + 1) * tq, cols(h)] = y.T.astype(o_ref.dtype)


def _attention(zb, lam4, sub_g_col, cast_ws, *, bsz, seq, lambda_init, tq=1024, tk=256, hg=2):
    gw = hg * DA_V_DIM
    kq, kk, kv = ZB_AQ // gw, ZB_AK // gw, ZB_AV // gw
    vt_rows = DA_V_DIM + 16
    groups = DA_HEADS // hg
    steps = bsz * groups
    slab = lambda b, g: (b * groups + g, 0)
    cast_specs = [pl.BlockSpec((w.shape[0] // steps, w.shape[1]), slab) for w in cast_ws]
    assert all(w.shape[0] % (16 * steps) == 0 for w in cast_ws)
    return pl.pallas_call(
        functools.partial(_attn_kernel, tq=tq, tk=tk, hg=hg, lambda_init=lambda_init,
                          n_cast=len(cast_ws)),
        grid=(bsz, groups),
        in_specs=[
            pl.BlockSpec((4, DA_QK_DIM), lambda b, g: (0, 0)),
            pl.BlockSpec((seq, gw), lambda b, g: (b, kq + g)),
            pl.BlockSpec((seq, gw), lambda b, g: (b, kk + g)),
            pl.BlockSpec((seq, gw), lambda b, g: (b, kv + g)),
            pl.BlockSpec((DA_V_DIM, 1), lambda b, g: (0, 0)),
            *cast_specs,
        ],
        out_specs=[pl.BlockSpec((seq, gw), lambda b, g: (b, g)), *cast_specs],
        out_shape=[jax.ShapeDtypeStruct((bsz * seq, DA_WIDTH), BF16),
                   *[jax.ShapeDtypeStruct(w.shape, BF16) for w in cast_ws]],
        scratch_shapes=[
            pltpu.VMEM((hg, seq // tk, vt_rows, tk), BF16),
            pltpu.VMEM((2 * hg, tk, tq), F32),
            pltpu.VMEM((2 * hg, tk, tq), F32),
            pltpu.VMEM((2 * hg, vt_rows, tq), F32),
        ],
        compiler_params=_params("parallel", "parallel"),
        name="diff_attn",
    )(lam4, zb, zb, zb, sub_g_col, *cast_ws)


def _log_sigmoid(x):
    return jnp.minimum(x, 0.0) - jnp.log(1.0 + jnp.exp(-jnp.abs(x)))


def _mlstm_kernel(bias_ref, q_ref, k_ref, v_ref, og_ref, zif_ref, g_ref, y_ref, c_ref, *, seq):
    L = CHUNK
    d = ML_HEAD_DIM
    h = pl.program_id(1)
    b_i = bias_ref[h]
    b_f = bias_ref[ML_HEADS + h]
    c_ref[...] = jnp.zeros(c_ref.shape, F32)

    lane = lax.broadcasted_iota(jnp.int32, (L, LANES), 1)
    sel_i = lane == h
    sel_f = lane == h + ML_HEADS
    row = lax.broadcasted_iota(jnp.int32, (L, L), 0)
    col = lax.broadcasted_iota(jnp.int32, (L, L), 1)
    causal = col <= row
    eye = col == row
    qscale = d ** -0.5

    def chunk(c, carry):
        m_prev, n_prev = carry
        t0 = pl.multiple_of(c * L, L)
        z = zif_ref[pl.ds(t0, L), :]
        i_col = jnp.sum(jnp.where(sel_i, z, 0.0), axis=1, keepdims=True) + b_i
        f_col = _log_sigmoid(jnp.sum(jnp.where(sel_f, z, 0.0), axis=1, keepdims=True) + b_f)
        f_mat = jnp.broadcast_to(f_col, (L, L))
        i_mat = jnp.broadcast_to(i_col, (L, L))
        b_row = jnp.sum(jnp.where(row <= col, f_mat, 0.0), axis=0, keepdims=True)
        i_row = jnp.sum(jnp.where(eye, i_mat, 0.0), axis=0, keepdims=True)
        b_col = jnp.sum(jnp.where(eye, jnp.broadcast_to(b_row, (L, L)), 0.0), axis=1, keepdims=True)

        dlog = jnp.where(causal, b_col - b_row + i_row, NEG_INF)
        inter_log = b_col + m_prev
        m_t = jnp.maximum(inter_log, jnp.max(dlog, axis=1, keepdims=True))
        dmat = jnp.exp(dlog - m_t)
        inter_w = jnp.exp(inter_log - m_t)

        qb = q_ref[pl.ds(t0, L), :]
        kb = k_ref[pl.ds(t0, L), :]
        q = qb.astype(F32)
        k = kb.astype(F32)
        v = v_ref[pl.ds(t0, L), :]

        s = (_dot_nt(qb, kb) * qscale) * dmat
        inter = _dot(qb, c_ref[...].astype(BF16)) * qscale
        num = _dot(s.astype(BF16), v) + inter_w * inter
        qn = jnp.sum(q * n_prev, axis=1, keepdims=True) * qscale
        den = jnp.sum(s, axis=1, keepdims=True) + inter_w * qn
        hh = num / jnp.maximum(jnp.abs(den), jnp.exp(-m_t))

        hn = _rms(hh, g_ref[...])
        og = og_ref[pl.ds(t0, L), :].astype(F32)
        y_ref[pl.ds(t0, L), :] = (jax.nn.sigmoid(og) * hn).astype(y_ref.dtype)

        b_last = b_col[L - 1:L, :]
        upd = b_last - b_col + i_col
        m_new = jnp.maximum(b_last + m_prev, jnp.max(upd, axis=0, keepdims=True))
        w_col = jnp.exp(upd - m_new)
        decay = jnp.exp(b_last + m_prev - m_new)
        kw = k * w_col
        c_ref[...] = decay * c_ref[...] + _dot_tn(kw.astype(BF16), v)
        n_new = decay * n_prev + jnp.sum(kw, axis=0, keepdims=True)
        return m_new, n_new

    lax.fori_loop(0, seq // L, chunk, (jnp.zeros((1, 1), F32), jnp.zeros((1, d), F32)), unroll=True)


def _mlstm(zb, zqk, zif, bias_if, norm_g, *, bsz, seq):
    d = ML_HEAD_DIM
    kmv, kmo = ZB_MV // d, ZB_MO // d
    return pl.pallas_call(
        functools.partial(_mlstm_kernel, seq=seq),
        grid=(bsz, ML_HEADS),
        in_specs=[
            pl.BlockSpec(memory_space=pltpu.SMEM),
            pl.BlockSpec((seq, d), lambda b, h: (b, h)),
            pl.BlockSpec((seq, d), lambda b, h: (b, ML_HEADS + h)),
            pl.BlockSpec((seq, d), lambda b, h: (b, kmv + h)),
            pl.BlockSpec((seq, d), lambda b, h: (b, kmo + h)),
            pl.BlockSpec((seq, LANES), lambda b, h: (b, 0)),
            pl.BlockSpec((1, d), lambda b, h: (0, h)),
        ],
        out_specs=pl.BlockSpec((seq, d), lambda b, h: (b, h)),
        out_shape=jax.ShapeDtypeStruct((bsz * seq, ML_WIDTH), BF16),
        scratch_shapes=[pltpu.VMEM((d, d), F32)],
        compiler_params=_params("parallel", "parallel"),
        name="mlstm",
    )(bias_if, zqk, zqk, zb, zb, zif, norm_g)


def _merge_kernel(ya_ref, yb_ref, ga_ref, gb_ref, x_ref, wpa_ref, wpb_ref, wo_ref, wup_ref, wdn_ref,
                  o_ref, wup_bf_ref, wdn_bf_ref):
    pa = _dot(ya_ref[...], wpa_ref[...])
    pb = _dot(yb_ref[...], wpb_ref[...])
    merged = (jax.nn.sigmoid(ga_ref[...].astype(F32)) * pa
              + jax.nn.sigmoid(gb_ref[...].astype(F32)) * pb)
    o_ref[...] = x_ref[...] + _dot(merged.astype(BF16), wo_ref[...])
    wup_bf_ref[...] = wup_ref[...].astype(BF16)
    wdn_bf_ref[...] = wdn_ref[...].astype(BF16)


def _resident(shape):
    return pl.BlockSpec(shape, lambda *_: (0,) * len(shape), pipeline_mode=pl.Buffered(1))


def _merge(ya, yb, zb, x2, w_pa, w_pb, w_o, w_up, w_down, *, tm=256):
    t = x2.shape[0]
    steps = t // tm
    fw = D_FF // steps
    assert fw % LANES == 0
    kga, kgb = ZB_GA // D_MODEL, ZB_GB // D_MODEL
    return pl.pallas_call(
        _merge_kernel,
        grid=(steps,),
        in_specs=[
            pl.BlockSpec((tm, DA_WIDTH), lambda i: (i, 0)),
            pl.BlockSpec((tm, ML_WIDTH), lambda i: (i, 0)),
            pl.BlockSpec((tm, D_MODEL), lambda i: (i, kga)),
            pl.BlockSpec((tm, D_MODEL), lambda i: (i, kgb)),
            pl.BlockSpec((tm, D_MODEL), lambda i: (i, 0)),
            _resident((DA_WIDTH, D_MODEL)),
            _resident((ML_WIDTH, D_MODEL)),
            _resident((D_MODEL, D_MODEL)),
            pl.BlockSpec((D_MODEL, fw), lambda i: (0, i)),
            pl.BlockSpec((fw, D_MODEL), lambda i: (i, 0)),
        ],
        out_specs=[
            pl.BlockSpec((tm, D_MODEL), lambda i: (i, 0)),
            pl.BlockSpec((D_MODEL, fw), lambda i: (0, i)),
            pl.BlockSpec((fw, D_MODEL), lambda i: (i, 0)),
        ],
        out_shape=[
            jax.ShapeDtypeStruct((t, D_MODEL), F32),
            jax.ShapeDtypeStruct((D_MODEL, D_FF), BF16),
            jax.ShapeDtypeStruct((D_FF, D_MODEL), BF16),
        ],
        compiler_params=_params("parallel"),
        name="merge",
    )(ya, yb, zb, zb, x2, w_pa, w_pb, w_o, w_up, w_down)


def _mlp_kernel(x_ref, g_ref, wup_ref, wdn_ref, o_ref, h_ref):
    f = pl.program_id(1)

    @pl.when(f == 0)
    def _():
        x = x_ref[...]
        h_ref[...] = _rms(x, g_ref[...]).astype(BF16)
        o_ref[...] = x

    u = _dot(h_ref[...], wup_ref[...])
    a = jnp.square(jnp.maximum(u, 0.0)).astype(BF16)
    o_ref[...] += _dot(a, wdn_ref[...])


def _mlp(x1, g_mlp, w_up, w_down, *, tm=1024, tf=1024):
    t = x1.shape[0]
    return pl.pallas_call(
        _mlp_kernel,
        grid=(t // tm, D_FF // tf),
        in_specs=[
            pl.BlockSpec((tm, D_MODEL), lambda i, f: (i, 0)),
            pl.BlockSpec((1, D_MODEL), lambda i, f: (0, 0)),
            pl.BlockSpec((D_MODEL, tf), lambda i, f: (0, f)),
            pl.BlockSpec((tf, D_MODEL), lambda i, f: (f, 0)),
        ],
        out_specs=pl.BlockSpec((tm, D_MODEL), lambda i, f: (i, 0)),
        out_shape=jax.ShapeDtypeStruct((t, D_MODEL), F32),
        scratch_shapes=[pltpu.VMEM((tm, D_MODEL), BF16)],
        compiler_params=_params("parallel", "arbitrary", vmem_limit_bytes=MLP_VMEM_LIMIT_BYTES),
        name="mlp",
    )(x1, g_mlp, w_up, w_down)


def _ple_kernel(x_ref, p_ref, gp_ref, wg_ref, wp_ref, gf_ref, o_ref):
    x = x_ref[...]
    gate = jax.nn.sigmoid(_dot(_rms(x, gp_ref[...]).astype(BF16), wg_ref[...]))
    proj = _dot(p_ref[...].astype(BF16), wp_ref[...])
    o_ref[...] = _rms(x + gate * proj, gf_ref[...])


def _ple(x2, p2, g_ple, w_gate, w_proj, g_final, *, tm=512):
    t = x2.shape[0]
    return pl.pallas_call(
        _ple_kernel,
        grid=(t // tm,),
        in_specs=[
            pl.BlockSpec((tm, D_MODEL), lambda i: (i, 0)),
            pl.BlockSpec((tm, PLE_DIM), lambda i: (i, 0)),
            pl.BlockSpec((1, D_MODEL), lambda i: (0, 0)),
            _resident((D_MODEL, D_MODEL)),
            _resident((PLE_DIM, D_MODEL)),
            pl.BlockSpec((1, D_MODEL), lambda i: (0, 0)),
        ],
        out_specs=pl.BlockSpec((tm, D_MODEL), lambda i: (i, 0)),
        out_shape=jax.ShapeDtypeStruct((t, D_MODEL), F32),
        compiler_params=_params("parallel"),
        name="ple_final",
    )(x2, p2, g_ple, w_gate, w_proj, g_final)


def kernel(x, p, g_mix, w_in, conv_w, conv_b, b_i, b_f, lam_q1, lam_k1, lam_q2, lam_k2,
           da_sub_g, ml_norm_g, w_pa, w_pb, w_o, g_mlp, w_up, w_down, g_ple, w_ple_gate,
           w_ple_proj, g_final):
    bsz, seq, _ = x.shape
    t = bsz * seq
    assert w_in.shape[0] == 1, "single-layer block only"
    lambda_init = 0.8 - 0.6 * math.exp(-0.3 * 0)
    x2 = x.reshape(t, D_MODEL)
    row = lambda v: v.reshape(1, -1).astype(F32)

    w_a = jnp.swapaxes(w_in[0], 0, 1).astype(BF16)
    c_if = 3 * DA_WIDTH + 4 * ML_WIDTH
    c_gate = c_if + 2 * ML_HEADS
    w_g = w_a[c_gate:]
    w_if = jnp.pad(w_a[c_if:c_gate], ((0, LANES - 2 * ML_HEADS), (0, 0)))
    bias_if = jnp.concatenate([b_i[0], b_f[0]]).astype(F32)
    lam4 = jnp.stack([lam_q1[0], lam_k1[0], lam_q2[0], lam_k2[0]]).astype(F32)

    zb, zqk, zif = _in_proj(x2, row(g_mix[0]), w_a, w_g, w_if, conv_w[0].astype(F32),
                            row(conv_b[0]), seq=seq)
    ya, w_pa_bf, w_pb_bf, w_o_bf, w_pg_bf = _attention(
        zb, lam4, da_sub_g[0].reshape(DA_V_DIM, 1).astype(F32),
        [w_pa[0], w_pb[0], w_o[0], w_ple_gate[0]], bsz=bsz, seq=seq, lambda_init=lambda_init)
    yb = _mlstm(zb, zqk, zif, bias_if, row(ml_norm_g[0]), bsz=bsz, seq=seq)
    x2, w_up_bf, w_down_bf = _merge(ya, yb, zb, x2, w_pa_bf, w_pb_bf, w_o_bf, w_up[0], w_down[0])
    x2 = _mlp(x2, row(g_mlp[0]), w_up_bf, w_down_bf)
    x2 = _ple(x2, p[0].reshape(t, PLE_DIM), row(g_ple[0]), w_pg_bf,
              w_ple_proj[0].astype(BF16), row(g_final))
    return x2.reshape(bsz, seq, D_MODEL)
```
